```python
import math
import jax, jax.numpy as jnp
from jax import lax
import numpy as np

D_MODEL = 1024
BATCH = 32
SEQ = 2048
DEPTH = 2
DEC_BATCH = 16
DEC_SEQ = 32
PAST_LEN = 2048

CHUNK = 64
Q_BLOCK = 128
ROPE_THETA = 10000.0
EPS = 1e-6
DA_HEAD_DIM = 64
DA_V_DIM = 2 * DA_HEAD_DIM
DA_HEADS = D_MODEL // DA_V_DIM
V_HEAD_DIM = 128
MLA_HEADS = D_MODEL // V_HEAD_DIM
NOPE_DIM = 128
ROPE_DIM = 64
Q_LORA = 384
KV_LORA = 256
MLA_SCALE = (NOPE_DIM + ROPE_DIM) ** -0.5
D_FF = 2816
CONV_W = 3
DA_Q_COLS = DA_HEADS * 2 * DA_HEAD_DIM
DA_K_COLS = DA_HEADS * 2 * DA_HEAD_DIM
DA_V_COLS = DA_HEADS * DA_V_DIM
GATE_COLS = 2 * D_MODEL
IN_COLS = DA_Q_COLS + DA_K_COLS + DA_V_COLS + Q_LORA + KV_LORA + ROPE_DIM + GATE_COLS

kernel_name = "diffattn_mla_convffn_streaming_step"


def rmsnorm(x, g):
    xf = x.astype(jnp.float32)
    y = xf * lax.rsqrt(jnp.mean(xf * xf, axis=-1, keepdims=True) + EPS)
    return (y * g.astype(jnp.float32)).astype(x.dtype)


def rope(x, pos):
    half = x.shape[-1] // 2
    inv = jnp.power(ROPE_THETA, -jnp.arange(half, dtype=jnp.float32) / half)
    ang = pos.astype(jnp.float32)[:, None] * inv[None, :]
    ang = ang.reshape((ang.shape[0],) + (1,) * (x.ndim - 3) + (half,))
    cos, sin = jnp.cos(ang), jnp.sin(ang)
    xf = x.astype(jnp.float32)
    x1, x2 = xf[..., :half], xf[..., half:]
    return jnp.concatenate([x1 * cos - x2 * sin, x2 * cos + x1 * sin], axis=-1).astype(x.dtype)


def chunk_mask(q_pos, k_pos):
    return (k_pos[None, :] // CHUNK) <= (q_pos[:, None] // CHUNK)


def masked_softmax(s, mask):
    return jax.nn.softmax(jnp.where(mask, s, -1e30), axis=-1)


def query_blocks(fn, qs, q_pos):
    S = q_pos.shape[0]
    if S <= Q_BLOCK or S % Q_BLOCK:
        return fn(qs, q_pos)
    nb = S // Q_BLOCK
    B = qs[0].shape[0]
    qs_b = tuple(jnp.moveaxis(q.reshape((B, nb, Q_BLOCK) + q.shape[2:]), 1, 0) for q in qs)
    out = lax.map(lambda a: fn(a[0], a[1]), (qs_b, q_pos.reshape(nb, Q_BLOCK)))
    out = jnp.moveaxis(out, 0, 1)
    return out.reshape((B, S) + out.shape[3:])


def diff_attention(q, q_pos, k, v, k_pos, lam):
    d = DA_HEAD_DIM
    scale = d ** -0.5
    mask = chunk_mask(q_pos, k_pos)
    s1 = jnp.einsum('bshd,bthd->bhst', q[..., :d], k[..., :d]).astype(jnp.float32) * scale
    s2 = jnp.einsum('bshd,bthd->bhst', q[..., d:], k[..., d:]).astype(jnp.float32) * scale
    a = (masked_softmax(s1, mask) - lam * masked_softmax(s2, mask)).astype(v.dtype)
    return jnp.einsum('bhst,bthv->bshv', a, v)


def latent_attention(q_lat, q_rope, q_pos, c_kv, k_rope, k_pos):
    mask = chunk_mask(q_pos, k_pos)
    s = (jnp.einsum('bshc,btc->bhst', q_lat, c_kv).astype(jnp.float32)
         + jnp.einsum('bshr,btr->bhst', q_rope, k_rope).astype(jnp.float32)) * MLA_SCALE
    p = masked_softmax(s, mask).astype(c_kv.dtype)
    return jnp.einsum('bhst,btc->bshc', p, c_kv)


def trunk_layer(x, past, lidx, w):
    (g_attn, w_in, b_gate, lam_q1, lam_k1, lam_q2, lam_k2, g_da_head, g_cq, w_uq, g_ckv,
     w_uk, w_uv, w_o, g_ffn, w_up, w_conv, b_conv, w_down) = w
    B, S, _ = x.shape
    past_len = 0 if past is None else past[0].shape[1]
    q_pos = past_len + jnp.arange(S, dtype=jnp.int32)
    k_pos = jnp.arange(past_len + S, dtype=jnp.int32)

    h = rmsnorm(x, g_attn)
    z = h @ w_in
    o1 = DA_Q_COLS
    o2 = o1 + DA_K_COLS
    o3 = o2 + DA_V_COLS
    o4 = o3 + Q_LORA
    o5 = o4 + KV_LORA
    o6 = o5 + ROPE_DIM
    q_da = rope(z[..., :o1].reshape(B, S, DA_HEADS, 2, DA_HEAD_DIM), q_pos).reshape(B, S, DA_HEADS, 2 * DA_HEAD_DIM)
    k_da = rope(z[..., o1:o2].reshape(B, S, DA_HEADS, 2, DA_HEAD_DIM), q_pos).reshape(B, S, DA_HEADS, 2 * DA_HEAD_DIM)
    v_da = z[..., o2:o3].reshape(B, S, DA_HEADS, DA_V_DIM)
    c_q = rmsnorm(z[..., o3:o4], g_cq)
    c_kv = rmsnorm(z[..., o4:o5], g_ckv)
    k_rope = rope(z[..., o5:o6], q_pos)
    gates = jax.nn.sigmoid(z[..., o6:] + b_gate)

    if past is None:
        k_all, v_all, ckv_all, kr_all = k_da, v_da, c_kv, k_rope
        prev_conv = jnp.zeros((B, CONV_W - 1, 2 * D_FF), x.dtype)
    else:
        k_all = jnp.concatenate([past[0], k_da], axis=1)
        v_all = jnp.concatenate([past[1], v_da], axis=1)
        ckv_all = jnp.concatenate([past[2], c_kv], axis=1)
        kr_all = jnp.concatenate([past[3], k_rope], axis=1)
        prev_conv = past[4]

    f32 = jnp.float32
    lam_init = 0.8 - 0.6 * math.exp(-0.3 * lidx)
    lam = (jnp.exp(jnp.sum(lam_q1.astype(f32) * lam_k1.astype(f32)))
           - jnp.exp(jnp.sum(lam_q2.astype(f32) * lam_k2.astype(f32))) + lam_init)
    o_da = query_blocks(lambda qs, qp: diff_attention(qs[0], qp, k_all, v_all, k_pos, lam), (q_da,), q_pos)
    o_da = (rmsnorm(o_da, g_da_head) * (1.0 - lam_init)).reshape(B, S, D_MODEL)

    q = jnp.einsum('bsc,chn->bshn', c_q, w_uq)
    q_rope = rope(q[..., NOPE_DIM:], q_pos)
    q_lat = jnp.einsum('bshn,hcn->bshc', q[..., :NOPE_DIM], w_uk)
    o_lat = query_blocks(lambda qs, qp: latent_attention(qs[0], qs[1], qp, ckv_all, kr_all, k_pos), (q_lat, q_rope), q_pos)
    o_mla = jnp.einsum('bshc,hcv->bshv', o_lat, w_uv).reshape(B, S, D_MODEL)

    x = x + (gates[..., :D_MODEL] * o_da + gates[..., D_MODEL:] * o_mla) @ w_o

    u = rmsnorm(x, g_ffn) @ w_up
    padded = jnp.concatenate([prev_conv, u], axis=1)
    c = b_conv + w_conv[0] * padded[:, 0:S]
    for j in range(1, CONV_W):
        c = c + w_conv[j] * padded[:, j:j + S]
    x = x + (jax.nn.silu(c[..., :D_FF]) * c[..., D_FF:]) @ w_down
    return x, (k_da, v_da, c_kv, k_rope, padded[:, S:])


def setup_inputs(seed: int = 0) -> dict:
    key = jax.random.key(seed)
    ks = iter(jax.random.split(key, 40))
    f32 = jnp.float32

    def nrm(shape, scale):
        return jax.random.normal(next(ks), shape, f32) * scale

    def gain(shape):
        return 1.0 + 0.01 * jax.random.normal(next(ks), shape, f32)

    L = DEPTH
    return {
        "x_prompt": nrm((BATCH, SEQ, D_MODEL), 1.0),
        "x_sample": nrm((DEC_BATCH, DEC_SEQ, D_MODEL), 1.0),
        "cache_dk": nrm((L, DEC_BATCH, PAST_LEN, DA_HEADS, 2 * DA_HEAD_DIM), 1.0),
        "cache_dv": nrm((L, DEC_BATCH, PAST_LEN, DA_HEADS, DA_V_DIM), 1.0),
        "cache_ckv": nrm((L, DEC_BATCH, PAST_LEN, KV_LORA), 1.0),
        "cache_krope": nrm((L, DEC_BATCH, PAST_LEN, ROPE_DIM), 1.0),
        "state_conv": nrm((L, DEC_BATCH, CONV_W - 1, 2 * D_FF), 1.0),
        "g_attn": gain((L, D_MODEL)),
        "w_in": nrm((L, D_MODEL, IN_COLS), D_MODEL ** -0.5),
        "b_gate": nrm((L, GATE_COLS), 0.01),
        "lam_q1": nrm((L, DA_HEAD_DIM), 0.1),
        "lam_k1": nrm((L, DA_HEAD_DIM), 0.1),
        "lam_q2": nrm((L, DA_HEAD_DIM), 0.1),
        "lam_k2": nrm((L, DA_HEAD_DIM), 0.1),
        "g_da_head": gain((L, DA_V_DIM)),
        "g_cq": gain((L, Q_LORA)),
        "w_uq": nrm((L, Q_LORA, MLA_HEADS, NOPE_DIM + ROPE_DIM), Q_LORA ** -0.5),
        "g_ckv": gain((L, KV_LORA)),
        "w_uk": nrm((L, MLA_HEADS, KV_LORA, NOPE_DIM), KV_LORA ** -0.5),
        "w_uv": nrm((L, MLA_HEADS, KV_LORA, V_HEAD_DIM), KV_LORA ** -0.5),
        "w_o": nrm((L, D_MODEL, D_MODEL), D_MODEL ** -0.5),
        "g_ffn": gain((L, D_MODEL)),
        "w_up": nrm((L, D_MODEL, 2 * D_FF), D_MODEL ** -0.5),
        "w_conv": nrm((L, CONV_W, 2 * D_FF), CONV_W ** -0.5),
        "b_conv": nrm((L, 2 * D_FF), 0.01),
        "w_down": nrm((L, D_FF, D_MODEL), D_FF ** -0.5),
        "g_final": gain((D_MODEL,)),
    }


def reference(x_prompt, x_sample, cache_dk, cache_dv, cache_ckv, cache_krope, state_conv,
              g_attn, w_in, b_gate, lam_q1, lam_k1, lam_q2, lam_k2, g_da_head, g_cq, w_uq,
              g_ckv, w_uk, w_uv, w_o, g_ffn, w_up, w_conv, b_conv, w_down, g_final):
    weights = (g_attn, w_in, b_gate, lam_q1, lam_k1, lam_q2, lam_k2, g_da_head, g_cq, w_uq,
               g_ckv, w_uk, w_uv, w_o, g_ffn, w_up, w_conv, b_conv, w_down)
    yp, ys = x_prompt, x_sample
    st_p, st_s = [], []
    for l in range(DEPTH):
        w = tuple(a[l] for a in weights)
        yp, sp = trunk_layer(yp, None, l, w)
        past = (cache_dk[l], cache_dv[l], cache_ckv[l], cache_krope[l], state_conv[l])
        ys, ss = trunk_layer(ys, past, l, w)
        st_p.append(sp)
        st_s.append(ss)
    y_prompt = rmsnorm(yp, g_final)
    y_sample = rmsnorm(ys, g_final)
    new_dk_p = jnp.stack([s[0] for s in st_p])
    new_dv_p = jnp.stack([s[1] for s in st_p])
    new_ckv_p = jnp.stack([s[2] for s in st_p])
    new_kr_p = jnp.stack([s[3] for s in st_p])
    new_conv_p = jnp.stack([s[4] for s in st_p])
    new_dk_s = jnp.stack([s[0] for s in st_s])
    new_dv_s = jnp.stack([s[1] for s in st_s])
    new_ckv_s = jnp.stack([s[2] for s in st_s])
    new_kr_s = jnp.stack([s[3] for s in st_s])
    new_conv_s = jnp.stack([s[4] for s in st_s])
    return (y_prompt, y_sample, new_dk_p, new_dv_p, new_ckv_p, new_kr_p, new_conv_p,
            new_dk_s, new_dv_s, new_ckv_s, new_kr_s, new_conv_s)
```

```python
import functools
import math

import numpy as np
import jax
import jax.numpy as jnp
from jax import lax
from jax.experimental import pallas as pl
from jax.experimental.pallas import tpu as pltpu

F32 = jnp.float32
BF16 = jnp.bfloat16

V7X_VMEM_BYTES = 64 * 1024 * 1024
VMEM_LIMIT_BYTES = V7X_VMEM_BYTES - 8 * 1024 * 1024

D_MODEL = 1024
CHUNK = 64
ROPE_THETA = 10000.0
EPS = 1e-6
DA_HEAD_DIM = 64
DA_V_DIM = 2 * DA_HEAD_DIM
DA_HEADS = D_MODEL // DA_V_DIM
V_HEAD_DIM = 128
MLA_HEADS = D_MODEL // V_HEAD_DIM
NOPE_DIM = 128
ROPE_DIM = 64
Q_LORA = 384
KV_LORA = 256
MLA_SCALE = (NOPE_DIM + ROPE_DIM) ** -0.5
DA_SCALE = DA_HEAD_DIM ** -0.5
D_FF = 2816
CONV_W = 3
HEADS = 8
assert DA_HEADS == HEADS and MLA_HEADS == HEADS
HEAD_W = 128
MLA_QK_W = 2 * HEAD_W
FF_CHUNK = 256
N_FF_CHUNKS = D_FF // FF_CHUNK
assert N_FF_CHUNKS * FF_CHUNK == D_FF

NEG_INF = float("-inf")
MASK_VALUE = -1e30


def _cparams(n_grid):
    return pltpu.CompilerParams(
        dimension_semantics=("arbitrary",) * n_grid,
        vmem_limit_bytes=VMEM_LIMIT_BYTES,
    )


def _const_spec(shape):
    nd = len(shape)
    return pl.BlockSpec(shape, lambda *_: (0,) * nd, pipeline_mode=pl.Buffered(1))


def _dot(a, b):
    return jnp.dot(a, b, preferred_element_type=F32)


def _dot_nt(a, b):
    return lax.dot_general(a, b, (((1,), (1,)), ((), ())), preferred_element_type=F32)


def _rms(x, g):
    return x * lax.rsqrt(jnp.mean(x * x, axis=-1, keepdims=True) + EPS) * g


N_STATE = 4


def _inproj_kernel(*refs, absorbed, aliased):
    n_in = 18 if absorbed else 19
    ins = refs[:n_in]
    outs = refs[n_in + (N_STATE if aliased else 0):]
    (x_ref, g_ref, wq_ref, wk_ref, wv_ref, wcq_ref, wckv_ref, wkr_ref, wg_ref, bg_ref,
     gcq_ref, gckv_ref, cos_ref, sa_ref, sb_ref, wuqn_ref, wuqr_ref) = ins[:17]
    kf_ref, vf_ref, ckv_ref, kr_ref = outs[:N_STATE]
    qda_ref, kb_ref, vb_ref, gate_ref, qm_ref = outs[N_STATE:N_STATE + 5]

    x = x_ref[0]
    h = _rms(x, g_ref[...]).astype(BF16)
    cos_t = cos_ref[...]
    sin_a = sa_ref[...]
    sin_b = sb_ref[...]

    def rope(t):
        return (t * cos_t + pltpu.roll(t, HEAD_W - 32, 1) * sin_a
                + pltpu.roll(t, 32, 1) * sin_b)

    def head(t, i):
        return t[:, i * HEAD_W:(i + 1) * HEAD_W]

    zq = _dot(h, wq_ref[...])
    for i in range(HEADS):
        qda_ref[0, i] = (rope(head(zq, i)) * DA_SCALE).astype(BF16)

    zk = _dot(h, wk_ref[...])
    for i in range(HEADS):
        r = rope(head(zk, i))
        kf_ref[0, 0, :, i * HEAD_W:(i + 1) * HEAD_W] = r
        kb_ref[0, i] = r.astype(BF16)

    zv = _dot(h, wv_ref[...])
    vf_ref[0, 0] = zv
    for i in range(HEADS):
        vb_ref[0, i] = head(zv, i).astype(BF16)

    cq = _rms(_dot(h, wcq_ref[...]), gcq_ref[...]).astype(BF16)
    ckv = _rms(_dot(h, wckv_ref[...]), gckv_ref[...])
    ckv_ref[0, 0] = ckv
    ckv_b = ckv.astype(BF16)

    kr = rope(_dot(h, wkr_ref[...]))
    kr_ref[0, 0] = kr[:, :ROPE_DIM]

    gate_ref[0] = jax.nn.sigmoid(_dot(h, wg_ref[...]) + bg_ref[...]).astype(BF16)

    qn = _dot(cq, wuqn_ref[...])
    qr = _dot(cq, wuqr_ref[...])
    if absorbed:
        wukt_ref = ins[17]
        for i in range(HEADS):
            ql = _dot(head(qn, i).astype(BF16), wukt_ref[i])
            qm_ref[0, i, :, :KV_LORA] = (ql * MLA_SCALE).astype(BF16)
            qm_ref[0, i, :, KV_LORA:] = (rope(head(qr, i)) * MLA_SCALE).astype(BF16)
    else:
        wukf_ref, wuvf_ref = ins[17:19]
        km_ref, vm_ref = outs[N_STATE + 5:N_STATE + 7]
        kn = _dot(ckv_b, wukf_ref[...])
        vm = _dot(ckv_b, wuvf_ref[...])
        kr_b = kr.astype(BF16)
        for i in range(HEADS):
            qm_ref[0, i, :, :HEAD_W] = (head(qn, i) * MLA_SCALE).astype(BF16)
            qm_ref[0, i, :, HEAD_W:] = (rope(head(qr, i)) * MLA_SCALE).astype(BF16)
            km_ref[0, i, :, :HEAD_W] = head(kn, i).astype(BF16)
            km_ref[0, i, :, HEAD_W:] = kr_b
            vm_ref[0, i] = head(vm, i).astype(BF16)


def _rope_tables(pos):
    half = ROPE_DIM // 2
    inv = jnp.power(ROPE_THETA, -jnp.arange(half, dtype=F32) / half)
    ang = pos.astype(F32)[:, None] * inv[None, :]
    cos, sin = jnp.cos(ang), jnp.sin(ang)
    zero = jnp.zeros_like(sin)
    reps = HEAD_W // ROPE_DIM
    cos_t = jnp.tile(jnp.concatenate([cos, cos], axis=1), (1, reps))
    sin_a = jnp.tile(jnp.concatenate([-sin, zero], axis=1), (1, reps))
    sin_b = jnp.tile(jnp.concatenate([zero, sin], axis=1), (1, reps))
    return cos_t, sin_a, sin_b


def _in_projection(x, pos, lw, layer, depth, state, *, absorbed, tm):
    B, S, _ = x.shape
    assert S % tm == 0
    ns = S // tm
    cos_t, sin_a, sin_b = _rope_tables(pos)

    tok = lambda w: pl.BlockSpec((1, tm, w), lambda b, s: (b, s, 0))
    hd = lambda w: pl.BlockSpec((1, HEADS, tm, w), lambda b, s: (b, 0, s, 0))
    st = lambda w: pl.BlockSpec((1, 1, tm, w), lambda b, s: (layer, b, s, 0))
    tab = pl.BlockSpec((tm, HEAD_W), lambda b, s: (s, 0))

    consts = [lw["g_attn"], lw["w_q"], lw["w_k"], lw["w_v"], lw["w_cq"], lw["w_ckv"], lw["w_kr"],
              lw["w_g"], lw["b_gate"], lw["g_cq"], lw["g_ckv"]]
    mla_w = [lw["w_uqn"], lw["w_uqr"]] + ([lw["w_ukt"]] if absorbed else [lw["w_ukf"], lw["w_uvf"]])
    ins = [x] + consts + [cos_t, sin_a, sin_b] + mla_w
    in_specs = ([tok(D_MODEL)] + [_const_spec(a.shape) for a in consts] + [tab, tab, tab]
                + [_const_spec(a.shape) for a in mla_w])
    aliases = {}
    if state is not None:
        for i, a in enumerate(state):
            aliases[len(ins)] = i
            ins.append(a)
            in_specs.append(pl.BlockSpec(memory_space=pl.ANY))

    sds = jax.ShapeDtypeStruct
    qm_w = KV_LORA + HEAD_W if absorbed else MLA_QK_W
    state_w = (D_MODEL, D_MODEL, KV_LORA, ROPE_DIM)
    out_shape = [sds((depth, B, S, w), F32) for w in state_w] + [
        sds((B, HEADS, S, HEAD_W), BF16),
        sds((B, HEADS, S, HEAD_W), BF16),
        sds((B, HEADS, S, HEAD_W), BF16),
        sds((B, S, 2 * D_MODEL), BF16),
        sds((B, HEADS, S, qm_w), BF16),
    ]
    out_specs = [st(w) for w in state_w] + [hd(HEAD_W), hd(HEAD_W), hd(HEAD_W),
                                             tok(2 * D_MODEL), hd(qm_w)]
    if not absorbed:
        out_shape += [sds((B, HEADS, S, MLA_QK_W), BF16), sds((B, HEADS, S, HEAD_W), BF16)]
        out_specs += [hd(MLA_QK_W), hd(HEAD_W)]

    outs = pl.pallas_call(
        functools.partial(_inproj_kernel, absorbed=absorbed, aliased=state is not None),
        grid=(B, ns),
        in_specs=in_specs,
        out_specs=out_specs,
        out_shape=out_shape,
        input_output_aliases=aliases,
        compiler_params=_cparams(2),
        name="in_projection_absorbed" if absorbed else "in_projection",
    )(*ins)
    return tuple(outs[:N_STATE]), tuple(outs[N_STATE:])


def _softmax_step(s, v, m_ref, l_ref, acc_ref):
    m_prev = m_ref[...]
    m_new = jnp.maximum(m_prev, jnp.max(s, axis=-1, keepdims=True))
    alpha = jnp.exp(m_prev - m_new)
    p = jnp.exp(s - m_new)
    l_ref[...] = alpha * l_ref[...] + jnp.sum(p, axis=-1, keepdims=True)
    acc_ref[...] = alpha * acc_ref[...] + _dot(p.astype(BF16), v)
    m_ref[...] = m_new


def _diag_mask(t):
    r = lax.broadcasted_iota(jnp.int32, (t, t), 0) // CHUNK
    c = lax.broadcasted_iota(jnp.int32, (t, t), 1) // CHUNK
    return c <= r


def _lambda(lq1_ref, lk1_ref, lq2_ref, lk2_ref, lam_init):
    a = jnp.sum(lq1_ref[...] * lk1_ref[...], axis=-1, keepdims=True)
    b = jnp.sum(lq2_ref[...] * lk2_ref[...], axis=-1, keepdims=True)
    return jnp.exp(a) - jnp.exp(b) + lam_init


def _diff_finish(o1, o2, lam, g, lam_init):
    o = o1 - lam * o2
    return _rms(o, g) * (1.0 - lam_init)


def _diff_attn_kernel(lq1_ref, lk1_ref, lq2_ref, lk2_ref, g_ref, q_ref, k_ref, v_ref, o_ref,
                      m1, l1, a1, m2, l2, a2, *, t, lam_init):
    qi = pl.program_id(2)
    q = q_ref[0, 0]
    lane = lax.broadcasted_iota(jnp.int32, q.shape, 1)
    zero = jnp.zeros_like(q)
    q1 = jnp.where(lane < DA_HEAD_DIM, q, zero)
    q2 = jnp.where(lane >= DA_HEAD_DIM, q, zero)
    for m, l, a in ((m1, l1, a1), (m2, l2, a2)):
        m[...] = jnp.full(m.shape, NEG_INF, F32)
        l[...] = jnp.zeros(l.shape, F32)
        a[...] = jnp.zeros(a.shape, F32)

    def step(j, mask):
        off = pl.multiple_of(j * t, t)
        k = k_ref[0, 0, pl.ds(off, t), :]
        v = v_ref[0, 0, pl.ds(off, t), :]
        for qq, m, l, a in ((q1, m1, l1, a1), (q2, m2, l2, a2)):
            s = _dot_nt(qq, k)
            if mask is not None:
                s = jnp.where(mask, s, MASK_VALUE)
            _softmax_step(s, v, m, l, a)

    def body(j, carry):
        step(j, None)
        return carry

    lax.fori_loop(0, qi, body, 0)
    step(qi, _diag_mask(t))

    lam = _lambda(lq1_ref, lk1_ref, lq2_ref, lk2_ref, lam_init)
    o = _diff_finish(a1[...] / l1[...], a2[...] / l2[...], lam, g_ref[...], lam_init)
    o_ref[0] = o.astype(BF16)


def _mla_attn_kernel(q_ref, k_ref, v_ref, o_ref, m, l, a, *, t):
    qi = pl.program_id(2)
    q = q_ref[0, 0]
    m[...] = jnp.full(m.shape, NEG_INF, F32)
    l[...] = jnp.zeros(l.shape, F32)
    a[...] = jnp.zeros(a.shape, F32)

    def step(j, mask):
        off = pl.multiple_of(j * t, t)
        s = _dot_nt(q, k_ref[0, 0, pl.ds(off, t), :])
        if mask is not None:
            s = jnp.where(mask, s, MASK_VALUE)
        _softmax_step(s, v_ref[0, 0, pl.ds(off, t), :], m, l, a)

    def body(j, carry):
        step(j, None)
        return carry

    lax.fori_loop(0, qi, body, 0)
    step(qi, _diag_mask(t))
    o_ref[0] = (a[...] / l[...]).astype(BF16)


def _attn_tile(S):
    t = min(S, 512)
    assert S % t == 0 and t % CHUNK == 0
    return t


def _prompt_diff_attention(q, k, v, lw, lam_init):
    B, H, S, _ = q.shape
    t = _attn_tile(S)
    qspec = pl.BlockSpec((1, 1, t, HEAD_W), lambda b, h, i: (b, h, i, 0))
    kvspec = pl.BlockSpec((1, 1, S, HEAD_W), lambda b, h, i: (b, h, 0, 0))
    small = [lw["lam_q1"], lw["lam_k1"], lw["lam_q2"], lw["lam_k2"], lw["g_da_head"]]
    stat = pltpu.VMEM((t, 1), F32)
    acc = pltpu.VMEM((t, HEAD_W), F32)
    return pl.pallas_call(
        functools.partial(_diff_attn_kernel, t=t, lam_init=lam_init),
        grid=(B, H, S // t),
        in_specs=[_const_spec(a.shape) for a in small] + [qspec, kvspec, kvspec],
        out_specs=pl.BlockSpec((1, t, HEAD_W), lambda b, h, i: (b, i, h)),
        out_shape=jax.ShapeDtypeStruct((B, S, D_MODEL), BF16),
        scratch_shapes=[stat, stat, acc, stat, stat, acc],
        compiler_params=_cparams(3),
        name="diff_attention",
    )(*small, q, k, v)


def _prompt_mla_attention(q, k, v):
    B, H, S, _ = q.shape
    t = _attn_tile(S)
    return pl.pallas_call(
        functools.partial(_mla_attn_kernel, t=t),
        grid=(B, H, S // t),
        in_specs=[pl.BlockSpec((1, 1, t, MLA_QK_W), lambda b, h, i: (b, h, i, 0)),
                  pl.BlockSpec((1, 1, S, MLA_QK_W), lambda b, h, i: (b, h, 0, 0)),
                  pl.BlockSpec((1, 1, S, HEAD_W), lambda b, h, i: (b, h, 0, 0))],
        out_specs=pl.BlockSpec((1, t, HEAD_W), lambda b, h, i: (b, i, h)),
        out_shape=jax.ShapeDtypeStruct((B, S, D_MODEL), BF16),
        scratch_shapes=[pltpu.VMEM((t, 1), F32), pltpu.VMEM((t, 1), F32),
                        pltpu.VMEM((t, HEAD_W), F32)],
        compiler_params=_cparams(3),
        name="mla_attention",
    )(q, k, v)


def _two_part_softmax(s_c, s_n):
    m = jnp.maximum(jnp.max(s_c, axis=-1, keepdims=True), jnp.max(s_n, axis=-1, keepdims=True))
    p_c = jnp.exp(s_c - m)
    p_n = jnp.exp(s_n - m)
    l = jnp.sum(p_c, axis=-1, keepdims=True) + jnp.sum(p_n, axis=-1, keepdims=True)
    return p_c.astype(BF16), p_n.astype(BF16), l


def _diff_decode_kernel(lq1_ref, lk1_ref, lq2_ref, lk2_ref, g_ref, q_ref, kc_ref, vc_ref,
                        kn_ref, vn_ref, o_ref, *, lam_init):
    q = q_ref[0, 0]
    lane = lax.broadcasted_iota(jnp.int32, q.shape, 1)
    zero = jnp.zeros_like(q)
    kc = kc_ref[0, 0].astype(BF16)
    vc = vc_ref[0, 0].astype(BF16)
    kn = kn_ref[0, 0]
    vn = vn_ref[0, 0]
    outs = []
    for qq in (jnp.where(lane < DA_HEAD_DIM, q, zero), jnp.where(lane >= DA_HEAD_DIM, q, zero)):
        p_c, p_n, l = _two_part_softmax(_dot_nt(qq, kc), _dot_nt(qq, kn))
        outs.append((_dot(p_c, vc) + _dot(p_n, vn)) / l)
    lam = _lambda(lq1_ref, lk1_ref, lq2_ref, lk2_ref, lam_init)
    o_ref[0] = _diff_finish(outs[0], outs[1], lam, g_ref[...], lam_init).astype(BF16)


def _sample_diff_attention(q, cache_k, cache_v, k_new, v_new, layer, lw, lam_init):
    B, H, S, _ = q.shape
    T = cache_k.shape[2]
    small = [lw["lam_q1"], lw["lam_k1"], lw["lam_q2"], lw["lam_k2"], lw["g_da_head"]]
    new = pl.BlockSpec((1, 1, S, HEAD_W), lambda b, h: (b, h, 0, 0))
    cache = pl.BlockSpec((1, 1, T, HEAD_W), lambda b, h: (layer, b, 0, h))
    return pl.pallas_call(
        functools.partial(_diff_decode_kernel, lam_init=lam_init),
        grid=(B, H),
        in_specs=[_const_spec(a.shape) for a in small] + [new, cache, cache, new, new],
        out_specs=pl.BlockSpec((1, S, HEAD_W), lambda b, h: (b, 0, h)),
        out_shape=jax.ShapeDtypeStruct((B, S, D_MODEL), BF16),
        compiler_params=_cparams(2),
        name="diff_attention_decode",
    )(*small, q, cache_k, cache_v, k_new, v_new)


def _mla_decode_kernel(q_ref, cc_ref, rc_ref, cn_ref, rn_ref, wuv_ref, o_ref, *, s_len):
    q = q_ref[0]
    ql = q[:, :KV_LORA]
    qr = q[:, KV_LORA:KV_LORA + ROPE_DIM]
    cc = cc_ref[0, 0].astype(BF16)
    cn = cn_ref[0, 0].astype(BF16)
    s_c = _dot_nt(ql, cc) + _dot_nt(qr, rc_ref[0, 0].astype(BF16))
    s_n = _dot_nt(ql, cn) + _dot_nt(qr, rn_ref[0, 0].astype(BF16))
    p_c, p_n, l = _two_part_softmax(s_c, s_n)
    o_lat = ((_dot(p_c, cc) + _dot(p_n, cn)) / l).astype(BF16)
    for i in range(HEADS):
        o_ref[0, :, i * HEAD_W:(i + 1) * HEAD_W] = _dot(
            o_lat[i * s_len:(i + 1) * s_len], wuv_ref[i]).astype(BF16)


def _sample_mla_attention(q_abs, cache_ckv, cache_kr, new_ckv, new_kr, layer, lw):
    B, H, S, W = q_abs.shape
    q2 = q_abs.reshape(B, H * S, W)
    per_layer = lambda a: pl.BlockSpec((1, 1) + a.shape[2:], lambda b: (layer, b, 0, 0))
    return pl.pallas_call(
        functools.partial(_mla_decode_kernel, s_len=S),
        grid=(B,),
        in_specs=[pl.BlockSpec((1, H * S, W), lambda b: (b, 0, 0)),
                  per_layer(cache_ckv), per_layer(cache_kr), per_layer(new_ckv), per_layer(new_kr),
                  _const_spec(lw["w_uv"].shape)],
        out_specs=pl.BlockSpec((1, S, D_MODEL), lambda b: (b, 0, 0)),
        out_shape=jax.ShapeDtypeStruct((B, S, D_MODEL), BF16),
        compiler_params=_cparams(1),
        name="mla_attention_decode",
    )(q2, cache_ckv, cache_kr, new_ckv, new_kr, lw["w_uv"])


N_UBUF = 4


def _merge_ffn_kernel(x_ref, oda_ref, omla_ref, gate_ref, prev_ref, wo_ref, gffn_ref, wup_ref,
                      wconv_ref, bconv_ref, wdown_ref, gfin_ref, y_ref, conv_ref,
                      carry, ubuf, act, *, tm, final):
    @pl.when(pl.program_id(1) == 0)
    def _():
        carry[...] = prev_ref[0]

    gates = gate_ref[0]
    merged = (gates[:, :D_MODEL].astype(F32) * oda_ref[0].astype(F32)
              + gates[:, D_MODEL:].astype(F32) * omla_ref[0].astype(F32))
    x1 = x_ref[0] + _dot(merged.astype(BF16), wo_ref[...])
    h = _rms(x1, gffn_ref[...]).astype(BF16)

    def conv_cols(c0, slot):
        cols = slice(c0, c0 + FF_CHUNK)
        u = _dot(h, wup_ref[:, cols])
        ubuf[slot, 6:8, :] = carry[:, cols]
        ubuf[slot, 8:tm + 8, :] = u
        tail = ubuf[slot, tm + 6:tm + 8, :]
        c = (bconv_ref[:, cols]
             + wconv_ref[0:1, cols] * ubuf[slot, 6:tm + 6, :]
             + wconv_ref[1:2, cols] * ubuf[slot, 7:tm + 7, :]
             + wconv_ref[2:3, cols] * u)
        carry[:, cols] = tail
        conv_ref[0, :, cols] = tail
        return c

    for j in range(N_FF_CHUNKS):
        ca = conv_cols(j * FF_CHUNK, (2 * j) % N_UBUF)
        cb = conv_cols(D_FF + j * FF_CHUNK, (2 * j + 1) % N_UBUF)
        act[:, j * FF_CHUNK:(j + 1) * FF_CHUNK] = (jax.nn.silu(ca) * cb).astype(BF16)

    x2 = x1 + _dot(act[...], wdown_ref[...])
    if final:
        x2 = _rms(x2, gfin_ref[...])
    y_ref[0] = x2


def _merge_ffn(x, o_da, o_mla, gates, prev_conv, lw, g_final, *, final, tm):
    B, S, _ = x.shape
    assert S % tm == 0 and tm >= CONV_W - 1
    tok = lambda w: pl.BlockSpec((1, tm, w), lambda b, s: (b, s, 0))
    state = pl.BlockSpec((1, CONV_W - 1, 2 * D_FF), lambda b, s: (b, 0, 0))
    consts = [lw["w_o"], lw["g_ffn"], lw["w_up"], lw["w_conv"], lw["b_conv"], lw["w_down"], g_final]
    return pl.pallas_call(
        functools.partial(_merge_ffn_kernel, tm=tm, final=final),
        grid=(B, S // tm),
        in_specs=[tok(D_MODEL), tok(D_MODEL), tok(D_MODEL), tok(2 * D_MODEL), state]
        + [_const_spec(a.shape) for a in consts],
        out_specs=[tok(D_MODEL), state],
        out_shape=[jax.ShapeDtypeStruct((B, S, D_MODEL), F32),
                   jax.ShapeDtypeStruct((B, CONV_W - 1, 2 * D_FF), F32)],
        scratch_shapes=[pltpu.VMEM((CONV_W - 1, 2 * D_FF), F32),
                        pltpu.VMEM((N_UBUF, tm + 8, FF_CHUNK), F32),
                        pltpu.VMEM((tm, D_FF), BF16)],
        compiler_params=_cparams(2),
        name="merge_ffn_final" if final else "merge_ffn",
    )(x, o_da, o_mla, gates, prev_conv, *consts)


def _layer_weights(l, g_attn, w_in, b_gate, lam_q1, lam_k1, lam_q2, lam_k2, g_da_head, g_cq, w_uq,
                   g_ckv, w_uk, w_uv, w_o, g_ffn, w_up, w_conv, b_conv, w_down):
    row = lambda a: a[l].reshape(1, -1).astype(F32)
    w = w_in[l]
    o1 = D_MODEL
    o2 = o1 + D_MODEL
    o3 = o2 + D_MODEL
    o4 = o3 + Q_LORA
    o5 = o4 + KV_LORA
    o6 = o5 + ROPE_DIM
    uq = w_uq[l]
    pad_rope = lambda a: jnp.pad(a, [(0, 0)] * (a.ndim - 1) + [(0, HEAD_W - ROPE_DIM)])
    return {
        "g_attn": row(g_attn),
        "w_q": w[:, :o1].astype(BF16),
        "w_k": w[:, o1:o2].astype(BF16),
        "w_v": w[:, o2:o3].astype(BF16),
        "w_cq": w[:, o3:o4].astype(BF16),
        "w_ckv": w[:, o4:o5].astype(BF16),
        "w_kr": pad_rope(w[:, o5:o6]).astype(BF16),
        "w_g": w[:, o6:].astype(BF16),
        "b_gate": row(b_gate),
        "g_cq": row(g_cq),
        "g_ckv": row(g_ckv),
        "w_uqn": uq[:, :, :NOPE_DIM].reshape(Q_LORA, HEADS * NOPE_DIM).astype(BF16),
        "w_uqr": pad_rope(uq[:, :, NOPE_DIM:]).reshape(Q_LORA, HEADS * HEAD_W).astype(BF16),
        "w_ukf": jnp.transpose(w_uk[l], (1, 0, 2)).reshape(KV_LORA, HEADS * NOPE_DIM).astype(BF16),
        "w_uvf": jnp.transpose(w_uv[l], (1, 0, 2)).reshape(KV_LORA, HEADS * V_HEAD_DIM).astype(BF16),
        "w_ukt": jnp.transpose(w_uk[l], (0, 2, 1)).astype(BF16),
        "w_uv": w_uv[l].astype(BF16),
        "lam_q1": row(lam_q1), "lam_k1": row(lam_k1), "lam_q2": row(lam_q2), "lam_k2": row(lam_k2),
        "g_da_head": row(g_da_head),
        "w_o": w_o[l].astype(BF16),
        "g_ffn": row(g_ffn),
        "w_up": w_up[l].astype(BF16),
        "w_conv": w_conv[l].astype(F32),
        "b_conv": row(b_conv),
        "w_down": w_down[l].astype(BF16),
    }


def _all_keys_visible(q_pos, k_pos):
    return bool(np.all((k_pos[None, :] // CHUNK) <= (q_pos[:, None] // CHUNK)))


def kernel(x_prompt, x_sample, cache_dk, cache_dv, cache_ckv, cache_krope, state_conv, g_attn, w_in, b_gate, lam_q1, lam_k1, lam_q2, lam_k2, g_da_head, g_cq, w_uq, g_ckv, w_uk, w_uv, w_o, g_ffn, w_up, w_conv, b_conv, w_down, g_final):
    depth = w_in.shape[0]
    B, S, _ = x_prompt.shape
    Bs, Ss, _ = x_sample.shape
    past = cache_dk.shape[2]
    assert _all_keys_visible(past + np.arange(Ss), np.arange(past + Ss))

    gfin = g_final.reshape(1, -1).astype(F32)
    pos_p = jnp.arange(S, dtype=jnp.int32)
    pos_s = past + jnp.arange(Ss, dtype=jnp.int32)
    tm_in = min(S, 256)
    tm_ffn = min(S, 512)
    zero_state = jnp.zeros((B, CONV_W - 1, 2 * D_FF), F32)
    cache_dk2 = cache_dk.reshape(depth, Bs, past, D_MODEL)
    cache_dv2 = cache_dv.reshape(depth, Bs, past, D_MODEL)

    yp, ys = x_prompt, x_sample
    state_p = state_s = None
    conv_p, conv_s = [], []
    for l in range(depth):
        lw = _layer_weights(l, g_attn, w_in, b_gate, lam_q1, lam_k1, lam_q2, lam_k2, g_da_head,
                            g_cq, w_uq, g_ckv, w_uk, w_uv, w_o, g_ffn, w_up, w_conv, b_conv, w_down)
        lam_init = 0.8 - 0.6 * math.exp(-0.3 * l)
        final = l == depth - 1

        state_p, (q_da, k_b, v_b, gates, q_m, k_m, v_m) = _in_projection(
            yp, pos_p, lw, l, depth, state_p, absorbed=False, tm=tm_in)
        o_da = _prompt_diff_attention(q_da, k_b, v_b, lw, lam_init)
        o_mla = _prompt_mla_attention(q_m, k_m, v_m)
        yp, cv = _merge_ffn(yp, o_da, o_mla, gates, zero_state, lw, gfin, final=final, tm=tm_ffn)
        conv_p.append(cv)

        state_s, (q_da, k_b, v_b, gates, q_abs) = _in_projection(
            ys, pos_s, lw, l, depth, state_s, absorbed=True, tm=Ss)
        o_da = _sample_diff_attention(q_da, cache_dk2, cache_dv2, k_b, v_b, l, lw, lam_init)
        o_mla = _sample_mla_attention(q_abs, cache_ckv, cache_krope, state_s[2], state_s[3], l, lw)
        ys, cv = _merge_ffn(ys, o_da, o_mla, gates, state_conv[l], lw, gfin, final=final, tm=Ss)
        conv_s.append(cv)

    dk_p, dv_p, ckv_p, kr_p = state_p
    dk_s, dv_s, ckv_s, kr_s = state_s
    heads = lambda a: a.reshape(a.shape[:3] + (DA_HEADS, DA_V_DIM))
    return (yp, ys, heads(dk_p), heads(dv_p), ckv_p, kr_p, jnp.stack(conv_p),
            heads(dk_s), heads(dv_s), ckv_s, kr_s, jnp.stack(conv_s))
```

```python
import functools
import math

import numpy as np
import jax
import jax.numpy as jnp
from jax import lax
from jax.experimental import pallas as pl
from jax.experimental.pallas import tpu as pltpu

F32 = jnp.float32
BF16 = jnp.bfloat16

V7X_VMEM_BYTES = 64 * 1024 * 1024
VMEM_LIMIT_BYTES = V7X_VMEM_BYTES - 8 * 1024 * 1024

D_MODEL = 1024
CHUNK = 64
ROPE_THETA = 10000.0
EPS = 1e-6
DA_HEAD_DIM = 64
DA_V_DIM = 2 * DA_HEAD_DIM
DA_HEADS = D_MODEL // DA_V_DIM
V_HEAD_DIM = 128
MLA_HEADS = D_MODEL // V_HEAD_DIM
NOPE_DIM = 128
ROPE_DIM = 64
Q_LORA = 384
KV_LORA = 256
MLA_SCALE = (NOPE_DIM + ROPE_DIM) ** -0.5
DA_SCALE = DA_HEAD_DIM ** -0.5
D_FF = 2816
CONV_W = 3
HEADS = 8
assert DA_HEADS == HEADS and MLA_HEADS == HEADS
HEAD_W = 128
MLA_QK_W = 2 * HEAD_W
FF_CHUNK = 256
N_FF_CHUNKS = D_FF // FF_CHUNK
assert N_FF_CHUNKS * FF_CHUNK == D_FF

LOG2E = math.log2(math.e)
MASK_VALUE = -1e30


def _cparams(n_grid):
    return pltpu.CompilerParams(
        dimension_semantics=("arbitrary",) * n_grid,
        vmem_limit_bytes=VMEM_LIMIT_BYTES,
    )


def _const_spec(shape):
    nd = len(shape)
    return pl.BlockSpec(shape, lambda *_: (0,) * nd, pipeline_mode=pl.Buffered(1))


def _dot(a, b):
    return jnp.dot(a, b, preferred_element_type=F32)


def _dot_nt(a, b):
    return lax.dot_general(a, b, (((1,), (1,)), ((), ())), preferred_element_type=F32)


def _rms(x, g):
    return x * lax.rsqrt(jnp.mean(x * x, axis=-1, keepdims=True) + EPS) * g


N_STATE = 4


def _inproj_kernel(*refs, absorbed, aliased, tm):
    n_in = 18 if absorbed else 19
    ins = refs[:n_in]
    outs = refs[n_in + (N_STATE if aliased else 0):]
    (x_ref, g_ref, wq_ref, wk_ref, wv_ref, wcq_ref, wckv_ref, wkr_ref, wg_ref, bg_ref,
     gcq_ref, gckv_ref, cos_ref, sa_ref, sb_ref, wuqn_ref, wuqr_ref) = ins[:17]
    kf_ref, vf_ref, ckv_ref, kr_ref = outs[:N_STATE]
    qda_ref, kb_ref, vb_ref, gate_ref, qm_ref = outs[N_STATE:N_STATE + 5]

    x = x_ref[0]
    h = _rms(x, g_ref[...]).astype(BF16)
    cos_t = cos_ref[...]
    sin_a = sa_ref[...]
    sin_b = sb_ref[...]

    def rope(t):
        return (t * cos_t + pltpu.roll(t, HEAD_W - 32, 1) * sin_a
                + pltpu.roll(t, 32, 1) * sin_b)

    def head(t, i):
        return t[:, i * HEAD_W:(i + 1) * HEAD_W]

    def put_v(ref, i, t):
        ref[0, i] = (t if absorbed else t.T).astype(BF16)

    zq = _dot(h, wq_ref[...])
    for i in range(HEADS):
        qda_ref[0, i] = (rope(head(zq, i)) * (DA_SCALE * LOG2E)).astype(BF16)

    zk = _dot(h, wk_ref[...])
    for i in range(HEADS):
        r = rope(head(zk, i))
        kf_ref[0, 0, pl.ds(i, tm, stride=HEADS), :] = r
        kb_ref[0, i] = r.astype(BF16)

    zv = _dot(h, wv_ref[...])
    for i in range(HEADS):
        vf_ref[0, 0, pl.ds(i, tm, stride=HEADS), :] = head(zv, i)
        put_v(vb_ref, i, head(zv, i))

    cq = _rms(_dot(h, wcq_ref[...]), gcq_ref[...]).astype(BF16)
    ckv = _rms(_dot(h, wckv_ref[...]), gckv_ref[...])
    ckv_ref[0, 0] = ckv
    ckv_b = ckv.astype(BF16)

    kr = rope(_dot(h, wkr_ref[...]))
    kr_ref[0, 0] = kr[:, :ROPE_DIM]

    gate_ref[0] = jax.nn.sigmoid(_dot(h, wg_ref[...]) + bg_ref[...]).astype(BF16)

    qn = _dot(cq, wuqn_ref[...])
    qr = _dot(cq, wuqr_ref[...])
    if absorbed:
        wukt_ref = ins[17]
        for i in range(HEADS):
            ql = _dot(head(qn, i).astype(BF16), wukt_ref[i])
            qm_ref[0, i, :, :KV_LORA] = (ql * (MLA_SCALE * LOG2E)).astype(BF16)
            qm_ref[0, i, :, KV_LORA:] = (rope(head(qr, i)) * (MLA_SCALE * LOG2E)).astype(BF16)
    else:
        wukf_ref, wuvf_ref = ins[17:19]
        km_ref, vm_ref = outs[N_STATE + 5:N_STATE + 7]
        kn = _dot(ckv_b, wukf_ref[...])
        vm = _dot(ckv_b, wuvf_ref[...])
        kr_b = kr.astype(BF16)
        for i in range(HEADS):
            qm_ref[0, i, :, :HEAD_W] = (head(qn, i) * (MLA_SCALE * LOG2E)).astype(BF16)
            qm_ref[0, i, :, HEAD_W:] = (rope(head(qr, i)) * (MLA_SCALE * LOG2E)).astype(BF16)
            km_ref[0, i, :, :HEAD_W] = head(kn, i).astype(BF16)
            km_ref[0, i, :, HEAD_W:] = kr_b
            put_v(vm_ref, i, head(vm, i))


def _rope_tables(pos):
    half = ROPE_DIM // 2
    inv = jnp.power(ROPE_THETA, -jnp.arange(half, dtype=F32) / half)
    ang = pos.astype(F32)[:, None] * inv[None, :]
    cos, sin = jnp.cos(ang), jnp.sin(ang)
    zero = jnp.zeros_like(sin)
    reps = HEAD_W // ROPE_DIM
    cos_t = jnp.tile(jnp.concatenate([cos, cos], axis=1), (1, reps))
    sin_a = jnp.tile(jnp.concatenate([-sin, zero], axis=1), (1, reps))
    sin_b = jnp.tile(jnp.concatenate([zero, sin], axis=1), (1, reps))
    return cos_t, sin_a, sin_b


def _in_projection(x, pos, lw, layer, depth, state, *, absorbed, tm):
    B, S, _ = x.shape
    assert S % tm == 0
    ns = S // tm
    cos_t, sin_a, sin_b = _rope_tables(pos)

    tok = lambda w: pl.BlockSpec((1, tm, w), lambda b, s: (b, s, 0))
    hd = lambda w: pl.BlockSpec((1, HEADS, tm, w), lambda b, s: (b, 0, s, 0))
    hdt = pl.BlockSpec((1, HEADS, HEAD_W, tm), lambda b, s: (b, 0, 0, s))
    vspec = hd(HEAD_W) if absorbed else hdt
    vshape = (B, HEADS, S, HEAD_W) if absorbed else (B, HEADS, HEAD_W, S)
    st = lambda rows, w: pl.BlockSpec((1, 1, rows, w), lambda b, s: (layer, b, s, 0))
    tab = pl.BlockSpec((tm, HEAD_W), lambda b, s: (s, 0))

    consts = [lw["g_attn"], lw["w_q"], lw["w_k"], lw["w_v"], lw["w_cq"], lw["w_ckv"], lw["w_kr"],
              lw["w_g"], lw["b_gate"], lw["g_cq"], lw["g_ckv"]]
    mla_w = [lw["w_uqn"], lw["w_uqr"]] + ([lw["w_ukt"]] if absorbed else [lw["w_ukf"], lw["w_uvf"]])
    ins = [x] + consts + [cos_t, sin_a, sin_b] + mla_w
    in_specs = ([tok(D_MODEL)] + [_const_spec(a.shape) for a in consts] + [tab, tab, tab]
                + [_const_spec(a.shape) for a in mla_w])
    aliases = {}
    if state is not None:
        for i, a in enumerate(state):
            aliases[len(ins)] = i
            ins.append(a)
            in_specs.append(pl.BlockSpec(memory_space=pl.ANY))

    sds = jax.ShapeDtypeStruct
    qm_w = KV_LORA + HEAD_W if absorbed else MLA_QK_W
    state_dims = ((S * HEADS, tm * HEADS, HEAD_W), (S * HEADS, tm * HEADS, HEAD_W),
                  (S, tm, KV_LORA), (S, tm, ROPE_DIM))
    out_shape = [sds((depth, B, rows, w), F32) for rows, _, w in state_dims] + [
        sds((B, HEADS, S, HEAD_W), BF16),
        sds((B, HEADS, S, HEAD_W), BF16),
        sds(vshape, BF16),
        sds((B, S, 2 * D_MODEL), BF16),
        sds((B, HEADS, S, qm_w), BF16),
    ]
    out_specs = [st(blk, w) for _, blk, w in state_dims] + [hd(HEAD_W), hd(HEAD_W), vspec,
                                                            tok(2 * D_MODEL), hd(qm_w)]
    if not absorbed:
        out_shape += [sds((B, HEADS, S, MLA_QK_W), BF16), sds(vshape, BF16)]
        out_specs += [hd(MLA_QK_W), vspec]

    outs = pl.pallas_call(
        functools.partial(_inproj_kernel, absorbed=absorbed, aliased=state is not None, tm=tm),
        grid=(B, ns),
        in_specs=in_specs,
        out_specs=out_specs,
        out_shape=out_shape,
        input_output_aliases=aliases,
        compiler_params=_cparams(2),
        name="in_projection_absorbed" if absorbed else "in_projection",
    )(*ins)
    return tuple(outs[:N_STATE]), tuple(outs[N_STATE:])


def _diag_mask_t(t):
    kc = lax.broadcasted_iota(jnp.int32, (t, t), 0) // CHUNK
    qc = lax.broadcasted_iota(jnp.int32, (t, t), 1) // CHUNK
    return kc <= qc


def _lambda(lq1_ref, lk1_ref, lq2_ref, lk2_ref, lam_init):
    a = jnp.sum(lq1_ref[...] * lk1_ref[...], axis=-1, keepdims=True)
    b = jnp.sum(lq2_ref[...] * lk2_ref[...], axis=-1, keepdims=True)
    return jnp.exp(a) - jnp.exp(b) + lam_init


def _diff_finish(o1, o2, lam, g, lam_init):
    o = o1 - lam * o2
    return _rms(o, g) * (1.0 - lam_init)


def _prompt_attn_kernel(*refs, t, n_q, diff, lam_init):
    if diff:
        lq1_ref, lk1_ref, lq2_ref, lk2_ref, g_ref, q_ref, k_ref, vt_ref, o_ref = refs
    else:
        q_ref, k_ref, vt_ref, o_ref = refs
    mask = _diag_mask_t(t)
    n_br = 2 if diff else 1
    queries = {}

    def branch_queries(qi):
        if qi not in queries:
            q = q_ref[0, 0, qi * t:(qi + 1) * t, :]
            if diff:
                lane = lax.broadcasted_iota(jnp.int32, q.shape, 1)
                zero = jnp.zeros_like(q)
                queries[qi] = (jnp.where(lane < DA_HEAD_DIM, q, zero),
                               jnp.where(lane >= DA_HEAD_DIM, q, zero))
            else:
                queries[qi] = (q,)
        return queries[qi]

    def scores(task):
        qi, j, b = task
        s = _dot_nt(k_ref[0, 0, j * t:(j + 1) * t, :], branch_queries(qi)[b])
        return jnp.where(mask, s, MASK_VALUE) if j == qi else s

    tasks = [(qi, j, b) for qi in range(n_q) for j in range(qi + 1) for b in range(n_br)]
    m = [None] * n_br
    l = [None] * n_br
    acc = [None] * n_br
    s_next = scores(tasks[0])
    for n, (qi, j, b) in enumerate(tasks):
        s = s_next
        if n + 1 < len(tasks):
            s_next = scores(tasks[n + 1])
        vt = vt_ref[0, 0, :, j * t:(j + 1) * t]
        s_max = jnp.max(s, axis=0, keepdims=True)
        if j == 0:
            m[b] = s_max
            p = jnp.exp2(s - m[b])
            l[b] = jnp.sum(p, axis=0, keepdims=True)
            acc[b] = _dot(vt, p.astype(BF16))
        else:
            m_new = jnp.maximum(m[b], s_max)
            alpha = jnp.exp2(m[b] - m_new)
            p = jnp.exp2(s - m_new)
            l[b] = alpha * l[b] + jnp.sum(p, axis=0, keepdims=True)
            acc[b] = alpha * acc[b] + _dot(vt, p.astype(BF16))
            m[b] = m_new
        if j == qi and b == n_br - 1:
            if diff:
                lam = _lambda(lq1_ref, lk1_ref, lq2_ref, lk2_ref, lam_init)
                o_t = acc[0] / l[0] - lam * (acc[1] / l[1])
                o_t = (o_t * lax.rsqrt(jnp.mean(o_t * o_t, axis=0, keepdims=True) + EPS)
                       * g_ref[...] * (1.0 - lam_init))
            else:
                o_t = acc[0] / l[0]
            o_ref[0, qi * t:(qi + 1) * t, :] = o_t.T.astype(BF16)


def _attn_tile(S):
    t = min(S, 512)
    assert S % t == 0 and t % CHUNK == 0
    return t


def _prompt_attention(q, k, vt, small, *, diff, lam_init, name):
    B, H, S, dq = q.shape
    t = _attn_tile(S)
    qk = pl.BlockSpec((1, 1, S, dq), lambda b, h: (b, h, 0, 0))
    return pl.pallas_call(
        functools.partial(_prompt_attn_kernel, t=t, n_q=S // t, diff=diff, lam_init=lam_init),
        grid=(B, H),
        in_specs=[_const_spec(a.shape) for a in small]
        + [qk, qk, pl.BlockSpec((1, 1, HEAD_W, S), lambda b, h: (b, h, 0, 0))],
        out_specs=pl.BlockSpec((1, S, HEAD_W), lambda b, h: (b, 0, h)),
        out_shape=jax.ShapeDtypeStruct((B, S, D_MODEL), BF16),
        compiler_params=_cparams(2),
        name=name,
    )(*small, q, k, vt)


def _prompt_diff_attention(q, k, vt, lw, lam_init):
    small = [lw["lam_q1"], lw["lam_k1"], lw["lam_q2"], lw["lam_k2"], lw["g_da_col"]]
    return _prompt_attention(q, k, vt, small, diff=True, lam_init=lam_init, name="diff_attention")


def _prompt_mla_attention(q, k, vt):
    return _prompt_attention(q, k, vt, [], diff=False, lam_init=None, name="mla_attention")


def _two_part_softmax(s_c, s_n):
    m = jnp.maximum(jnp.max(s_c, axis=-1, keepdims=True), jnp.max(s_n, axis=-1, keepdims=True))
    p_c = jnp.exp2(s_c - m)
    p_n = jnp.exp2(s_n - m)
    l = jnp.sum(p_c, axis=-1, keepdims=True) + jnp.sum(p_n, axis=-1, keepdims=True)
    return p_c.astype(BF16), p_n.astype(BF16), l


def _diff_decode_kernel(lq1_ref, lk1_ref, lq2_ref, lk2_ref, g_ref, q_ref, kc_ref, vc_ref,
                        kn_ref, vn_ref, o_ref, *, lam_init):
    q = q_ref[0, 0]
    lane = lax.broadcasted_iota(jnp.int32, q.shape, 1)
    zero = jnp.zeros_like(q)
    kc = kc_ref[0, 0].astype(BF16)
    vc = vc_ref[0, 0].astype(BF16)
    kn = kn_ref[0, 0]
    vn = vn_ref[0, 0]
    outs = []
    for qq in (jnp.where(lane < DA_HEAD_DIM, q, zero), jnp.where(lane >= DA_HEAD_DIM, q, zero)):
        p_c, p_n, l = _two_part_softmax(_dot_nt(qq, kc), _dot_nt(qq, kn))
        outs.append((_dot(p_c, vc) + _dot(p_n, vn)) / l)
    lam = _lambda(lq1_ref, lk1_ref, lq2_ref, lk2_ref, lam_init)
    o_ref[0] = _diff_finish(outs[0], outs[1], lam, g_ref[...], lam_init).astype(BF16)


def _sample_diff_attention(q, cache_k, cache_v, k_new, v_new, layer, lw, lam_init):
    B, H, S, _ = q.shape
    T = cache_k.shape[2]
    small = [lw["lam_q1"], lw["lam_k1"], lw["lam_q2"], lw["lam_k2"], lw["g_da_head"]]
    new = pl.BlockSpec((1, 1, S, HEAD_W), lambda b, h: (b, h, 0, 0))
    cache = pl.BlockSpec((1, 1, T, HEAD_W), lambda b, h: (layer, b, 0, h))
    return pl.pallas_call(
        functools.partial(_diff_decode_kernel, lam_init=lam_init),
        grid=(B, H),
        in_specs=[_const_spec(a.shape) for a in small] + [new, cache, cache, new, new],
        out_specs=pl.BlockSpec((1, S, HEAD_W), lambda b, h: (b, 0, h)),
        out_shape=jax.ShapeDtypeStruct((B, S, D_MODEL), BF16),
        compiler_params=_cparams(2),
        name="diff_attention_decode",
    )(*small, q, cache_k, cache_v, k_new, v_new)


def _mla_decode_kernel(q_ref, cc_ref, rc_ref, cn_ref, rn_ref, wuv_ref, o_ref, *, s_len):
    q = q_ref[0]
    ql = q[:, :KV_LORA]
    qr = q[:, KV_LORA:KV_LORA + ROPE_DIM]
    cc = cc_ref[0, 0].astype(BF16)
    cn = cn_ref[0, 0].astype(BF16)
    s_c = _dot_nt(ql, cc) + _dot_nt(qr, rc_ref[0, 0].astype(BF16))
    s_n = _dot_nt(ql, cn) + _dot_nt(qr, rn_ref[0, 0].astype(BF16))
    p_c, p_n, l = _two_part_softmax(s_c, s_n)
    o_lat = ((_dot(p_c, cc) + _dot(p_n, cn)) / l).astype(BF16)
    for i in range(HEADS):
        o_ref[0, :, i * HEAD_W:(i + 1) * HEAD_W] = _dot(
            o_lat[i * s_len:(i + 1) * s_len], wuv_ref[i]).astype(BF16)


def _sample_mla_attention(q_abs, cache_ckv, cache_kr, new_ckv, new_kr, layer, lw):
    B, H, S, W = q_abs.shape
    q2 = q_abs.reshape(B, H * S, W)
    per_layer = lambda a: pl.BlockSpec((1, 1) + a.shape[2:], lambda b: (layer, b, 0, 0))
    return pl.pallas_call(
        functools.partial(_mla_decode_kernel, s_len=S),
        grid=(B,),
        in_specs=[pl.BlockSpec((1, H * S, W), lambda b: (b, 0, 0)),
                  per_layer(cache_ckv), per_layer(cache_kr), per_layer(new_ckv), per_layer(new_kr),
                  _const_spec(lw["w_uv"].shape)],
        out_specs=pl.BlockSpec((1, S, D_MODEL), lambda b: (b, 0, 0)),
        out_shape=jax.ShapeDtypeStruct((B, S, D_MODEL), BF16),
        compiler_params=_cparams(1),
        name="mla_attention_decode",
    )(q2, cache_ckv, cache_kr, new_ckv, new_kr, lw["w_uv"])


N_UBUF = 4


def _merge_ffn_kernel(x_ref, oda_ref, omla_ref, gate_ref, prev_ref, wo_ref, gffn_ref, wup_ref,
                      wconv_ref, bconv_ref, wdown_ref, gfin_ref, y_ref, conv_ref,
                      carry, ubuf, act, *, tm, final):
    @pl.when(pl.program_id(1) == 0)
    def _():
        carry[...] = prev_ref[0]

    gates = gate_ref[0]
    merged = (gates[:, :D_MODEL].astype(F32) * oda_ref[0].astype(F32)
              + gates[:, D_MODEL:].astype(F32) * omla_ref[0].astype(F32))
    x1 = x_ref[0] + _dot(merged.astype(BF16), wo_ref[...])
    h = _rms(x1, gffn_ref[...]).astype(BF16)

    def conv_cols(c0, slot):
        cols = slice(c0, c0 + FF_CHUNK)
        u = _dot(h, wup_ref[:, cols])
        ubuf[slot, 6:8, :] = carry[:, cols]
        ubuf[slot, 8:tm + 8, :] = u
        tail = ubuf[slot, tm + 6:tm + 8, :]
        c = (bconv_ref[:, cols]
             + wconv_ref[0:1, cols] * ubuf[slot, 6:tm + 6, :]
             + wconv_ref[1:2, cols] * ubuf[slot, 7:tm + 7, :]
             + wconv_ref[2:3, cols] * u)
        carry[:, cols] = tail
        conv_ref[0, :, cols] = tail
        return c

    for j in range(N_FF_CHUNKS):
        ca = conv_cols(j * FF_CHUNK, (2 * j) % N_UBUF)
        cb = conv_cols(D_FF + j * FF_CHUNK, (2 * j + 1) % N_UBUF)
        act[:, j * FF_CHUNK:(j + 1) * FF_CHUNK] = (jax.nn.silu(ca) * cb).astype(BF16)

    x2 = x1 + _dot(act[...], wdown_ref[...])
    if final:
        x2 = _rms(x2, gfin_ref[...])
    y_ref[0] = x2


def _merge_ffn(x, o_da, o_mla, gates, prev_conv, lw, g_final, *, final, tm):
    B, S, _ = x.shape
    assert S % tm == 0 and tm >= CONV_W - 1
    tok = lambda w: pl.BlockSpec((1, tm, w), lambda b, s: (b, s, 0))
    state = pl.BlockSpec((1, CONV_W - 1, 2 * D_FF), lambda b, s: (b, 0, 0))
    consts = [lw["w_o"], lw["g_ffn"], lw["w_up"], lw["w_conv"], lw["b_conv"], lw["w_down"], g_final]
    return pl.pallas_call(
        functools.partial(_merge_ffn_kernel, tm=tm, final=final),
        grid=(B, S // tm),
        in_specs=[tok(D_MODEL), tok(D_MODEL), tok(D_MODEL), tok(2 * D_MODEL), state]
        + [_const_spec(a.shape) for a in consts],
        out_specs=[tok(D_MODEL), state],
        out_shape=[jax.ShapeDtypeStruct((B, S, D_MODEL), F32),
                   jax.ShapeDtypeStruct((B, CONV_W - 1, 2 * D_FF), F32)],
        scratch_shapes=[pltpu.VMEM((CONV_W - 1, 2 * D_FF), F32),
                        pltpu.VMEM((N_UBUF, tm + 8, FF_CHUNK), F32),
                        pltpu.VMEM((tm, D_FF), BF16)],
        compiler_params=_cparams(2),
        name="merge_ffn_final" if final else "merge_ffn",
    )(x, o_da, o_mla, gates, prev_conv, *consts)


def _layer_weights(l, g_attn, w_in, b_gate, lam_q1, lam_k1, lam_q2, lam_k2, g_da_head, g_cq, w_uq,
                   g_ckv, w_uk, w_uv, w_o, g_ffn, w_up, w_conv, b_conv, w_down):
    row = lambda a: a[l].reshape(1, -1).astype(F32)
    w = w_in[l]
    o1 = D_MODEL
    o2 = o1 + D_MODEL
    o3 = o2 + D_MODEL
    o4 = o3 + Q_LORA
    o5 = o4 + KV_LORA
    o6 = o5 + ROPE_DIM
    uq = w_uq[l]
    pad_rope = lambda a: jnp.pad(a, [(0, 0)] * (a.ndim - 1) + [(0, HEAD_W - ROPE_DIM)])
    return {
        "g_attn": row(g_attn),
        "w_q": w[:, :o1].astype(BF16),
        "w_k": w[:, o1:o2].astype(BF16),
        "w_v": w[:, o2:o3].astype(BF16),
        "w_cq": w[:, o3:o4].astype(BF16),
        "w_ckv": w[:, o4:o5].astype(BF16),
        "w_kr": pad_rope(w[:, o5:o6]).astype(BF16),
        "w_g": w[:, o6:].astype(BF16),
        "b_gate": row(b_gate),
        "g_cq": row(g_cq),
        "g_ckv": row(g_ckv),
        "w_uqn": uq[:, :, :NOPE_DIM].reshape(Q_LORA, HEADS * NOPE_DIM).astype(BF16),
        "w_uqr": pad_rope(uq[:, :, NOPE_DIM:]).reshape(Q_LORA, HEADS * HEAD_W).astype(BF16),
        "w_ukf": jnp.transpose(w_uk[l], (1, 0, 2)).reshape(KV_LORA, HEADS * NOPE_DIM).astype(BF16),
        "w_uvf": jnp.transpose(w_uv[l], (1, 0, 2)).reshape(KV_LORA, HEADS * V_HEAD_DIM).astype(BF16),
        "w_ukt": jnp.transpose(w_uk[l], (0, 2, 1)).astype(BF16),
        "w_uv": w_uv[l].astype(BF16),
        "lam_q1": row(lam_q1), "lam_k1": row(lam_k1), "lam_q2": row(lam_q2), "lam_k2": row(lam_k2),
        "g_da_head": row(g_da_head),
        "g_da_col": g_da_head[l].reshape(-1, 1).astype(F32),
        "w_o": w_o[l].astype(BF16),
        "g_ffn": row(g_ffn),
        "w_up": w_up[l].astype(BF16),
        "w_conv": w_conv[l].astype(F32),
        "b_conv": row(b_conv),
        "w_down": w_down[l].astype(BF16),
    }


def _all_keys_visible(q_pos, k_pos):
    return bool(np.all((k_pos[None, :] // CHUNK) <= (q_pos[:, None] // CHUNK)))


def kernel(x_prompt, x_sample, cache_dk, cache_dv, cache_ckv, cache_krope, state_conv, g_attn, w_in, b_gate, lam_q1, lam_k1, lam_q2, lam_k2, g_da_head, g_cq, w_uq, g_ckv, w_uk, w_uv, w_o, g_ffn, w_up, w_conv, b_conv, w_down, g_final):
    depth = w_in.shape[0]
    B, S, _ = x_prompt.shape
    Bs, Ss, _ = x_sample.shape
    past = cache_dk.shape[2]
    assert _all_keys_visible(past + np.arange(Ss), np.arange(past + Ss))

    gfin = g_final.reshape(1, -1).astype(F32)
    pos_p = jnp.arange(S, dtype=jnp.int32)
    pos_s = past + jnp.arange(Ss, dtype=jnp.int32)
    tm_in = min(S, 256)
    tm_ffn = min(S, 512)
    zero_state = jnp.zeros((B, CONV_W - 1, 2 * D_FF), F32)
    cache_dk2 = cache_dk.reshape(depth, Bs, past, D_MODEL)
    cache_dv2 = cache_dv.reshape(depth, Bs, past, D_MODEL)

    yp, ys = x_prompt, x_sample
    state_p = state_s = None
    conv_p, conv_s = [], []
    for l in range(depth):
        lw = _layer_weights(l, g_attn, w_in, b_gate, lam_q1, lam_k1, lam_q2, lam_k2, g_da_head,
                            g_cq, w_uq, g_ckv, w_uk, w_uv, w_o, g_ffn, w_up, w_conv, b_conv, w_down)
        lam_init = 0.8 - 0.6 * math.exp(-0.3 * l)
        final = l == depth - 1

        state_p, (q_da, k_b, v_b, gates, q_m, k_m, v_m) = _in_projection(
            yp, pos_p, lw, l, depth, state_p, absorbed=False, tm=tm_in)
        o_da = _prompt_diff_attention(q_da, k_b, v_b, lw, lam_init)
        o_mla = _prompt_mla_attention(q_m, k_m, v_m)
        yp, cv = _merge_ffn(yp, o_da, o_mla, gates, zero_state, lw, gfin, final=final, tm=tm_ffn)
        conv_p.append(cv)

        state_s, (q_da, k_b, v_b, gates, q_abs) = _in_projection(
            ys, pos_s, lw, l, depth, state_s, absorbed=True, tm=Ss)
        o_da = _sample_diff_attention(q_da, cache_dk2, cache_dv2, k_b, v_b, l, lw, lam_init)
        o_mla = _sample_mla_attention(q_abs, cache_ckv, cache_krope, state_s[2], state_s[3], l, lw)
        ys, cv = _merge_ffn(ys, o_da, o_mla, gates, state_conv[l], lw, gfin, final=final, tm=Ss)
        conv_s.append(cv)

    dk_p, dv_p, ckv_p, kr_p = state_p
    dk_s, dv_s, ckv_s, kr_s = state_s
    heads = lambda a: a.reshape(a.shape[:2] + (-1, DA_HEADS, DA_V_DIM))
    return (yp, ys, heads(dk_p), heads(dv_p), ckv_p, kr_p, jnp.stack(conv_p),
            heads(dk_s), heads(dv_s), ckv_s, kr_s, jnp.stack(conv_s))
```

```python
import functools
import math

import numpy as np
import jax
import jax.numpy as jnp
from jax import lax
from jax.experimental import pallas as pl
from jax.experimental.pallas import tpu as pltpu

F32 = jnp.float32
BF16 = jnp.bfloat16

V7X_VMEM_BYTES = 64 * 1024 * 1024
VMEM_LIMIT_BYTES = V7X_VMEM_BYTES - 8 * 1024 * 1024

D_MODEL = 1024
CHUNK = 64
ROPE_THETA = 10000.0
EPS = 1e-6
DA_HEAD_DIM = 64
DA_V_DIM = 2 * DA_HEAD_DIM
DA_HEADS = D_MODEL // DA_V_DIM
V_HEAD_DIM = 128
MLA_HEADS = D_MODEL // V_HEAD_DIM
NOPE_DIM = 128
ROPE_DIM = 64
Q_LORA = 384
KV_LORA = 256
MLA_SCALE = (NOPE_DIM + ROPE_DIM) ** -0.5
DA_SCALE = DA_HEAD_DIM ** -0.5
D_FF = 2816
CONV_W = 3
HEADS = 8
assert DA_HEADS == HEADS and MLA_HEADS == HEADS
HEAD_W = 128
MLA_QK_W = 2 * HEAD_W
BF16_SUBLANES = 16
V_ROWS = HEAD_W + BF16_SUBLANES
FF_CHUNK = 256
N_FF_CHUNKS = D_FF // FF_CHUNK
assert N_FF_CHUNKS * FF_CHUNK == D_FF

LOG2E = math.log2(math.e)
MASK_VALUE = -1e30


def _cparams(n_grid):
    return pltpu.CompilerParams(
        dimension_semantics=("arbitrary",) * n_grid,
        vmem_limit_bytes=VMEM_LIMIT_BYTES,
    )


def _const_spec(shape):
    nd = len(shape)
    return pl.BlockSpec(shape, lambda *_: (0,) * nd, pipeline_mode=pl.Buffered(1))


def _dot(a, b):
    return jnp.dot(a, b, preferred_element_type=F32)


def _dot_nt(a, b):
    return lax.dot_general(a, b, (((1,), (1,)), ((), ())), preferred_element_type=F32)


def _rms(x, g):
    return x * lax.rsqrt(jnp.mean(x * x, axis=-1, keepdims=True) + EPS) * g


N_STATE = 4


def _inproj_kernel(*refs, absorbed, aliased, tm):
    n_in = 18 if absorbed else 19
    ins = refs[:n_in]
    outs = refs[n_in + (N_STATE if aliased else 0):]
    (x_ref, g_ref, wq_ref, wk_ref, wv_ref, wcq_ref, wckv_ref, wkr_ref, wg_ref, bg_ref,
     gcq_ref, gckv_ref, cos_ref, sa_ref, sb_ref, wuqn_ref, wuqr_ref) = ins[:17]
    kf_ref, vf_ref, ckv_ref, kr_ref = outs[:N_STATE]
    qda_ref, kb_ref, vb_ref, gate_ref, qm_ref = outs[N_STATE:N_STATE + 5]

    x = x_ref[0]
    h = _rms(x, g_ref[...]).astype(BF16)
    cos_t = cos_ref[...]
    sin_a = sa_ref[...]
    sin_b = sb_ref[...]

    def rope(t):
        return (t * cos_t + pltpu.roll(t, HEAD_W - 32, 1) * sin_a
                + pltpu.roll(t, 32, 1) * sin_b)

    def head(t, i):
        return t[:, i * HEAD_W:(i + 1) * HEAD_W]

    def put_v(ref, i, t):
        if absorbed:
            ref[0, i] = t.astype(BF16)
        else:
            ref[0, i, :HEAD_W, :] = t.T.astype(BF16)
            ref[0, i, HEAD_W:, :] = jnp.ones((V_ROWS - HEAD_W, tm), BF16)

    zq = _dot(h, wq_ref[...])
    for i in range(HEADS):
        qda_ref[0, i] = (rope(head(zq, i)) * (DA_SCALE * LOG2E)).astype(BF16)

    zk = _dot(h, wk_ref[...])
    for i in range(HEADS):
        r = rope(head(zk, i))
        kf_ref[0, 0, pl.ds(i, tm, stride=HEADS), :] = r
        kb_ref[0, i] = r.astype(BF16)

    zv = _dot(h, wv_ref[...])
    for i in range(HEADS):
        vf_ref[0, 0, pl.ds(i, tm, stride=HEADS), :] = head(zv, i)
        put_v(vb_ref, i, head(zv, i))

    cq = _rms(_dot(h, wcq_ref[...]), gcq_ref[...]).astype(BF16)
    ckv = _rms(_dot(h, wckv_ref[...]), gckv_ref[...])
    ckv_ref[0, 0] = ckv
    ckv_b = ckv.astype(BF16)

    kr = rope(_dot(h, wkr_ref[...]))
    kr_ref[0, 0] = kr[:, :ROPE_DIM]

    gate_ref[0] = jax.nn.sigmoid(_dot(h, wg_ref[...]) + bg_ref[...]).astype(BF16)

    qn = _dot(cq, wuqn_ref[...])
    qr = _dot(cq, wuqr_ref[...])
    if absorbed:
        wukt_ref = ins[17]
        for i in range(HEADS):
            ql = _dot(head(qn, i).astype(BF16), wukt_ref[i])
            qm_ref[0, i, :, :KV_LORA] = (ql * (MLA_SCALE * LOG2E)).astype(BF16)
            qm_ref[0, i, :, KV_LORA:] = (rope(head(qr, i)) * (MLA_SCALE * LOG2E)).astype(BF16)
    else:
        wukf_ref, wuvf_ref = ins[17:19]
        km_ref, vm_ref = outs[N_STATE + 5:N_STATE + 7]
        kn = _dot(ckv_b, wukf_ref[...])
        vm = _dot(ckv_b, wuvf_ref[...])
        kr_b = kr.astype(BF16)
        for i in range(HEADS):
            qm_ref[0, i, :, :HEAD_W] = (head(qn, i) * (MLA_SCALE * LOG2E)).astype(BF16)
            qm_ref[0, i, :, HEAD_W:] = (rope(head(qr, i)) * (MLA_SCALE * LOG2E)).astype(BF16)
            km_ref[0, i, :, :HEAD_W] = head(kn, i).astype(BF16)
            km_ref[0, i, :, HEAD_W:] = kr_b
            put_v(vm_ref, i, head(vm, i))


def _rope_tables(pos):
    half = ROPE_DIM // 2
    inv = jnp.power(ROPE_THETA, -jnp.arange(half, dtype=F32) / half)
    ang = pos.astype(F32)[:, None] * inv[None, :]
    cos, sin = jnp.cos(ang), jnp.sin(ang)
    zero = jnp.zeros_like(sin)
    reps = HEAD_W // ROPE_DIM
    cos_t = jnp.tile(jnp.concatenate([cos, cos], axis=1), (1, reps))
    sin_a = jnp.tile(jnp.concatenate([-sin, zero], axis=1), (1, reps))
    sin_b = jnp.tile(jnp.concatenate([zero, sin], axis=1), (1, reps))
    return cos_t, sin_a, sin_b


def _in_projection(x, pos, lw, layer, depth, state, *, absorbed, tm):
    B, S, _ = x.shape
    assert S % tm == 0
    ns = S // tm
    cos_t, sin_a, sin_b = _rope_tables(pos)

    tok = lambda w: pl.BlockSpec((1, tm, w), lambda b, s: (b, s, 0))
    hd = lambda w: pl.BlockSpec((1, HEADS, tm, w), lambda b, s: (b, 0, s, 0))
    hdt = pl.BlockSpec((1, HEADS, V_ROWS, tm), lambda b, s: (b, 0, 0, s))
    vspec = hd(HEAD_W) if absorbed else hdt
    vshape = (B, HEADS, S, HEAD_W) if absorbed else (B, HEADS, V_ROWS, S)
    st = lambda rows, w: pl.BlockSpec((1, 1, rows, w), lambda b, s: (layer, b, s, 0))
    tab = pl.BlockSpec((tm, HEAD_W), lambda b, s: (s, 0))

    consts = [lw["g_attn"], lw["w_q"], lw["w_k"], lw["w_v"], lw["w_cq"], lw["w_ckv"], lw["w_kr"],
              lw["w_g"], lw["b_gate"], lw["g_cq"], lw["g_ckv"]]
    mla_w = [lw["w_uqn"], lw["w_uqr"]] + ([lw["w_ukt"]] if absorbed else [lw["w_ukf"], lw["w_uvf"]])
    ins = [x] + consts + [cos_t, sin_a, sin_b] + mla_w
    in_specs = ([tok(D_MODEL)] + [_const_spec(a.shape) for a in consts] + [tab, tab, tab]
                + [_const_spec(a.shape) for a in mla_w])
    aliases = {}
    if state is not None:
        for i, a in enumerate(state):
            aliases[len(ins)] = i
            ins.append(a)
            in_specs.append(pl.BlockSpec(memory_space=pl.ANY))

    sds = jax.ShapeDtypeStruct
    qm_w = KV_LORA + HEAD_W if absorbed else MLA_QK_W
    state_dims = ((S * HEADS, tm * HEADS, HEAD_W), (S * HEADS, tm * HEADS, HEAD_W),
                  (S, tm, KV_LORA), (S, tm, ROPE_DIM))
    out_shape = [sds((depth, B, rows, w), F32) for rows, _, w in state_dims] + [
        sds((B, HEADS, S, HEAD_W), BF16),
        sds((B, HEADS, S, HEAD_W), BF16),
        sds(vshape, BF16),
        sds((B, S, 2 * D_MODEL), BF16),
        sds((B, HEADS, S, qm_w), BF16),
    ]
    out_specs = [st(blk, w) for _, blk, w in state_dims] + [hd(HEAD_W), hd(HEAD_W), vspec,
                                                            tok(2 * D_MODEL), hd(qm_w)]
    if not absorbed:
        out_shape += [sds((B, HEADS, S, MLA_QK_W), BF16), sds(vshape, BF16)]
        out_specs += [hd(MLA_QK_W), vspec]

    outs = pl.pallas_call(
        functools.partial(_inproj_kernel, absorbed=absorbed, aliased=state is not None, tm=tm),
        grid=(B, ns),
        in_specs=in_specs,
        out_specs=out_specs,
        out_shape=out_shape,
        input_output_aliases=aliases,
        compiler_params=_cparams(2),
        name="in_projection_absorbed" if absorbed else "in_projection",
    )(*ins)
    return tuple(outs[:N_STATE]), tuple(outs[N_STATE:])


def _lambda(lq1_ref, lk1_ref, lq2_ref, lk2_ref, lam_init):
    a = jnp.sum(lq1_ref[...] * lk1_ref[...], axis=-1, keepdims=True)
    b = jnp.sum(lq2_ref[...] * lk2_ref[...], axis=-1, keepdims=True)
    return jnp.exp(a) - jnp.exp(b) + lam_init


def _diff_finish(o1, o2, lam, g, lam_init):
    o = o1 - lam * o2
    return _rms(o, g) * (1.0 - lam_init)


def _key_blocks(qi, tq, tk):
    n_full = (tq * qi) // tk
    blocks = [(tk * jj, tk) for jj in range(n_full)]
    blocks.append((tk * n_full, tq * (qi + 1) - tk * n_full))
    return blocks


def _prompt_attn_kernel(*refs, tq, tk, lookahead, n_q, diff, lam_init):
    if diff:
        lq1_ref, lk1_ref, lq2_ref, lk2_ref, g_ref, q_ref, k_ref, vt_ref, o_ref = refs
    else:
        q_ref, k_ref, vt_ref, o_ref = refs
    n_br = 2 if diff else 1
    queries = {}
    masks = {}

    def branch_queries(qi):
        if qi not in queries:
            q = q_ref[0, 0, qi * tq:(qi + 1) * tq, :]
            if diff:
                lane = lax.broadcasted_iota(jnp.int32, q.shape, 1)
                zero = jnp.zeros_like(q)
                queries[qi] = (jnp.where(lane < DA_HEAD_DIM, q, zero),
                               jnp.where(lane >= DA_HEAD_DIM, q, zero))
            else:
                queries[qi] = (q,)
        return queries[qi]

    def chunk_mask(size):
        if size not in masks:
            kc = lax.broadcasted_iota(jnp.int32, (size, tq), 0) // CHUNK
            qc = (lax.broadcasted_iota(jnp.int32, (size, tq), 1) + (size - tq)) // CHUNK
            masks[size] = kc <= qc
        return masks[size]

    def scores(task):
        qi, start, size, last, b = task
        s = _dot_nt(k_ref[0, 0, start:start + size, :], branch_queries(qi)[b])
        return jnp.where(chunk_mask(size), s, MASK_VALUE) if last else s

    tasks = []
    for qi in range(n_q):
        blocks = _key_blocks(qi, tq, tk)
        for start, size in blocks:
            for b in range(n_br):
                tasks.append((qi, start, size, (start, size) == blocks[-1], b))
    m = [None] * n_br
    acc = [None] * n_br
    pending = [scores(task) for task in tasks[:lookahead]]
    for n, (qi, start, size, last, b) in enumerate(tasks):
        s = pending.pop(0)
        if n + lookahead < len(tasks):
            pending.append(scores(tasks[n + lookahead]))
        vt = vt_ref[0, 0, :, start:start + size]
        s_max = jnp.max(s, axis=0, keepdims=True)
        if start == 0:
            m[b] = s_max
            acc[b] = _dot(vt, jnp.exp2(s - s_max).astype(BF16))
        else:
            m_new = jnp.maximum(m[b], s_max)
            acc[b] = jnp.exp2(m[b] - m_new) * acc[b] + _dot(vt, jnp.exp2(s - m_new).astype(BF16))
            m[b] = m_new
        if last and b == n_br - 1:
            outs = [a[:HEAD_W] / a[HEAD_W:HEAD_W + 1] for a in acc]
            if diff:
                lam = _lambda(lq1_ref, lk1_ref, lq2_ref, lk2_ref, lam_init)
                o_t = outs[0] - lam * outs[1]
                o_t = (o_t * lax.rsqrt(jnp.mean(o_t * o_t, axis=0, keepdims=True) + EPS)
                       * g_ref[...] * (1.0 - lam_init))
            else:
                o_t = outs[0]
            o_ref[0, qi * tq:(qi + 1) * tq, :] = o_t.T.astype(BF16)


ATTN_TQ = 512
ATTN_TK = 512
ATTN_LOOKAHEAD = 2


def _prompt_attention(q, k, vt, small, *, diff, lam_init, name):
    B, H, S, dq = q.shape
    tq, tk = min(S, ATTN_TQ), min(S, ATTN_TK)
    assert S % tq == 0 and S % tk == 0 and tk % tq == 0 and tq % CHUNK == 0
    qk = pl.BlockSpec((1, 1, S, dq), lambda b, h: (b, h, 0, 0))
    return pl.pallas_call(
        functools.partial(_prompt_attn_kernel, tq=tq, tk=tk, lookahead=ATTN_LOOKAHEAD,
                          n_q=S // tq, diff=diff, lam_init=lam_init),
        grid=(B, H),
        in_specs=[_const_spec(a.shape) for a in small]
        + [qk, qk, pl.BlockSpec((1, 1, V_ROWS, S), lambda b, h: (b, h, 0, 0))],
        out_specs=pl.BlockSpec((1, S, HEAD_W), lambda b, h: (b, 0, h)),
        out_shape=jax.ShapeDtypeStruct((B, S, D_MODEL), BF16),
        compiler_params=_cparams(2),
        name=name,
    )(*small, q, k, vt)


def _prompt_diff_attention(q, k, vt, lw, lam_init):
    small = [lw["lam_q1"], lw["lam_k1"], lw["lam_q2"], lw["lam_k2"], lw["g_da_col"]]
    return _prompt_attention(q, k, vt, small, diff=True, lam_init=lam_init, name="diff_attention")


def _prompt_mla_attention(q, k, vt):
    return _prompt_attention(q, k, vt, [], diff=False, lam_init=None, name="mla_attention")


def _two_part_softmax(s_c, s_n):
    m = jnp.maximum(jnp.max(s_c, axis=-1, keepdims=True), jnp.max(s_n, axis=-1, keepdims=True))
    p_c = jnp.exp2(s_c - m)
    p_n = jnp.exp2(s_n - m)
    l = jnp.sum(p_c, axis=-1, keepdims=True) + jnp.sum(p_n, axis=-1, keepdims=True)
    return p_c.astype(BF16), p_n.astype(BF16), l


def _diff_decode_kernel(lq1_ref, lk1_ref, lq2_ref, lk2_ref, g_ref, q_ref, kc_ref, vc_ref,
                        kn_ref, vn_ref, o_ref, *, lam_init, t_cache):
    lam = _lambda(lq1_ref, lk1_ref, lq2_ref, lk2_ref, lam_init)
    for h in range(HEADS):
        q = q_ref[0, h]
        lane = lax.broadcasted_iota(jnp.int32, q.shape, 1)
        zero = jnp.zeros_like(q)
        kc = kc_ref[0, 0, pl.ds(h, t_cache, stride=HEADS), :].astype(BF16)
        vc = vc_ref[0, 0, pl.ds(h, t_cache, stride=HEADS), :].astype(BF16)
        kn = kn_ref[0, h]
        vn = vn_ref[0, h]
        outs = []
        for qq in (jnp.where(lane < DA_HEAD_DIM, q, zero), jnp.where(lane >= DA_HEAD_DIM, q, zero)):
            p_c, p_n, l = _two_part_softmax(_dot_nt(qq, kc), _dot_nt(qq, kn))
            outs.append((_dot(p_c, vc) + _dot(p_n, vn)) / l)
        o_ref[0, :, h * HEAD_W:(h + 1) * HEAD_W] = _diff_finish(
            outs[0], outs[1], lam, g_ref[...], lam_init).astype(BF16)


def _sample_diff_attention(q, cache_k, cache_v, k_new, v_new, layer, lw, lam_init):
    B, H, S, _ = q.shape
    rows = cache_k.shape[2]
    small = [lw["lam_q1"], lw["lam_k1"], lw["lam_q2"], lw["lam_k2"], lw["g_da_head"]]
    new = pl.BlockSpec((1, H, S, HEAD_W), lambda b: (b, 0, 0, 0))
    cache = pl.BlockSpec((1, 1, rows, HEAD_W), lambda b: (layer, b, 0, 0))
    return pl.pallas_call(
        functools.partial(_diff_decode_kernel, lam_init=lam_init, t_cache=rows // H),
        grid=(B,),
        in_specs=[_const_spec(a.shape) for a in small] + [new, cache, cache, new, new],
        out_specs=pl.BlockSpec((1, S, D_MODEL), lambda b: (b, 0, 0)),
        out_shape=jax.ShapeDtypeStruct((B, S, D_MODEL), BF16),
        compiler_params=_cparams(1),
        name="diff_attention_decode",
    )(*small, q, cache_k, cache_v, k_new, v_new)


def _mla_decode_kernel(q_ref, cc_ref, rc_ref, cn_ref, rn_ref, wuv_ref, o_ref, *, s_len):
    q = q_ref[0]
    ql = q[:, :KV_LORA]
    qr = q[:, KV_LORA:KV_LORA + ROPE_DIM]
    cc = cc_ref[0, 0].astype(BF16)
    cn = cn_ref[0, 0].astype(BF16)
    s_c = _dot_nt(ql, cc) + _dot_nt(qr, rc_ref[0, 0].astype(BF16))
    s_n = _dot_nt(ql, cn) + _dot_nt(qr, rn_ref[0, 0].astype(BF16))
    p_c, p_n, l = _two_part_softmax(s_c, s_n)
    o_lat = ((_dot(p_c, cc) + _dot(p_n, cn)) / l).astype(BF16)
    for i in range(HEADS):
        o_ref[0, :, i * HEAD_W:(i + 1) * HEAD_W] = _dot(
            o_lat[i * s_len:(i + 1) * s_len], wuv_ref[i]).astype(BF16)


def _sample_mla_attention(q_abs, cache_ckv, cache_kr, new_ckv, new_kr, layer, lw):
    B, H, S, W = q_abs.shape
    q2 = q_abs.reshape(B, H * S, W)
    per_layer = lambda a: pl.BlockSpec((1, 1) + a.shape[2:], lambda b: (layer, b, 0, 0))
    return pl.pallas_call(
        functools.partial(_mla_decode_kernel, s_len=S),
        grid=(B,),
        in_specs=[pl.BlockSpec((1, H * S, W), lambda b: (b, 0, 0)),
                  per_layer(cache_ckv), per_layer(cache_kr), per_layer(new_ckv), per_layer(new_kr),
                  _const_spec(lw["w_uv"].shape)],
        out_specs=pl.BlockSpec((1, S, D_MODEL), lambda b: (b, 0, 0)),
        out_shape=jax.ShapeDtypeStruct((B, S, D_MODEL), BF16),
        compiler_params=_cparams(1),
        name="mla_attention_decode",
    )(q2, cache_ckv, cache_kr, new_ckv, new_kr, lw["w_uv"])


F32_SUBLANES = 8


def _merge_ffn_kernel(x_ref, oda_ref, omla_ref, gate_ref, prev_ref, wo_ref, gffn_ref, wup_ref,
                      wconv_ref, bconv_ref, wdown_ref, gfin_ref, y_ref, conv_ref,
                      carry, xbuf, act, *, tm, final):
    sub = F32_SUBLANES
    nv = tm // sub
    n_slab = D_MODEL // HEAD_W

    @pl.when(pl.program_id(1) == 0)
    def _():
        carry[...] = prev_ref[0]

    def to_slabs(val):
        for k in range(n_slab):
            xbuf[k] = val[:, k * HEAD_W:(k + 1) * HEAD_W]

    def from_slabs(row_tile):
        return jnp.concatenate(
            [jnp.concatenate([row_tile(k, c) for k in range(n_slab)], axis=1) for c in range(nv)],
            axis=0)

    gates = gate_ref[0]
    merged = (gates[:, :D_MODEL].astype(F32) * oda_ref[0].astype(F32)
              + gates[:, D_MODEL:].astype(F32) * omla_ref[0].astype(F32))
    to_slabs(x_ref[0] + _dot(merged.astype(BF16), wo_ref[...]))
    x1 = from_slabs(lambda k, c: xbuf[k, pl.ds(c, sub, stride=nv), :])
    h = _rms(x1, gffn_ref[...]).astype(BF16)
    first_row = lax.broadcasted_iota(jnp.int32, (sub, FF_CHUNK), 0) == 0

    def conv_cols(c0):
        cols = slice(c0, c0 + FF_CHUNK)
        u = _dot(h, wup_ref[:, cols])
        e1 = jnp.where(first_row, carry[1:2, cols], pltpu.roll(u[tm - sub:], 1, 0))
        e2 = jnp.where(first_row, carry[0:1, cols], pltpu.roll(u[tm - 2 * sub:tm - sub], 1, 0))
        prev1 = jnp.concatenate([e1, u[:tm - sub]], axis=0)
        prev2 = jnp.concatenate([e2, e1, u[:tm - 2 * sub]], axis=0)
        c = (bconv_ref[:, cols] + wconv_ref[0:1, cols] * prev2 + wconv_ref[1:2, cols] * prev1
             + wconv_ref[2:3, cols] * u)
        tail = jnp.concatenate([u[tm - sub - 1:tm - sub], u[tm - 1:tm]], axis=0)
        carry[:, cols] = tail
        conv_ref[0, :, cols] = tail
        return c

    for j in range(N_FF_CHUNKS):
        ca = conv_cols(j * FF_CHUNK)
        cb = conv_cols(D_FF + j * FF_CHUNK)
        act[:, j * FF_CHUNK:(j + 1) * FF_CHUNK] = (jax.nn.silu(ca) * cb).astype(BF16)

    x2 = x1 + _dot(act[...], wdown_ref[...])
    if final:
        x2 = _rms(x2, gfin_ref[...])
    to_slabs(x2)

    def natural_tile(k, tile):
        t0 = tile * sub
        if nv % sub == 0:
            return xbuf[k, pl.ds((t0 % nv) * sub + t0 // nv, sub, stride=sub), :]
        return jnp.concatenate(
            [xbuf[k, pl.ds(t0 // nv + g, nv, stride=sub), :] for g in range(sub // nv)], axis=0)

    y_ref[0] = from_slabs(natural_tile)


def _merge_ffn(x, o_da, o_mla, gates, prev_conv, lw, g_final, *, final, tm):
    B, S, _ = x.shape
    nv = tm // F32_SUBLANES
    assert S % tm == 0 and tm % BF16_SUBLANES == 0 and nv >= CONV_W - 1
    assert nv % F32_SUBLANES == 0 or F32_SUBLANES % nv == 0
    tok = lambda w: pl.BlockSpec((1, tm, w), lambda b, s: (b, s, 0))
    state = pl.BlockSpec((1, CONV_W - 1, 2 * D_FF), lambda b, s: (b, 0, 0))
    consts = [lw["w_o"], lw["g_ffn"], lw["w_up"], lw["w_conv"], lw["b_conv"], lw["w_down"], g_final]
    return pl.pallas_call(
        functools.partial(_merge_ffn_kernel, tm=tm, final=final),
        grid=(B, S // tm),
        in_specs=[tok(D_MODEL), tok(D_MODEL), tok(D_MODEL), tok(2 * D_MODEL), state]
        + [_const_spec(a.shape) for a in consts],
        out_specs=[tok(D_MODEL), state],
        out_shape=[jax.ShapeDtypeStruct((B, S, D_MODEL), F32),
                   jax.ShapeDtypeStruct((B, CONV_W - 1, 2 * D_FF), F32)],
        scratch_shapes=[pltpu.VMEM((CONV_W - 1, 2 * D_FF), F32),
                        pltpu.VMEM((D_MODEL // HEAD_W, tm, HEAD_W), F32),
                        pltpu.VMEM((tm, D_FF), BF16)],
        compiler_params=_cparams(2),
        name="merge_ffn_final" if final else "merge_ffn",
    )(x, o_da, o_mla, gates, prev_conv, *consts)


def _layer_weights(l, g_attn, w_in, b_gate, lam_q1, lam_k1, lam_q2, lam_k2, g_da_head, g_cq, w_uq,
                   g_ckv, w_uk, w_uv, w_o, g_ffn, w_up, w_conv, b_conv, w_down):
    row = lambda a: a[l].reshape(1, -1).astype(F32)
    w = w_in[l]
    o1 = D_MODEL
    o2 = o1 + D_MODEL
    o3 = o2 + D_MODEL
    o4 = o3 + Q_LORA
    o5 = o4 + KV_LORA
    o6 = o5 + ROPE_DIM
    uq = w_uq[l]
    pad_rope = lambda a: jnp.pad(a, [(0, 0)] * (a.ndim - 1) + [(0, HEAD_W - ROPE_DIM)])
    return {
        "g_attn": row(g_attn),
        "w_q": w[:, :o1].astype(BF16),
        "w_k": w[:, o1:o2].astype(BF16),
        "w_v": w[:, o2:o3].astype(BF16),
        "w_cq": w[:, o3:o4].astype(BF16),
        "w_ckv": w[:, o4:o5].astype(BF16),
        "w_kr": pad_rope(w[:, o5:o6]).astype(BF16),
        "w_g": w[:, o6:].astype(BF16),
        "b_gate": row(b_gate),
        "g_cq": row(g_cq),
        "g_ckv": row(g_ckv),
        "w_uqn": uq[:, :, :NOPE_DIM].reshape(Q_LORA, HEADS * NOPE_DIM).astype(BF16),
        "w_uqr": pad_rope(uq[:, :, NOPE_DIM:]).reshape(Q_LORA, HEADS * HEAD_W).astype(BF16),
        "w_ukf": jnp.transpose(w_uk[l], (1, 0, 2)).reshape(KV_LORA, HEADS * NOPE_DIM).astype(BF16),
        "w_uvf": jnp.transpose(w_uv[l], (1, 0, 2)).reshape(KV_LORA, HEADS * V_HEAD_DIM).astype(BF16),
        "w_ukt": jnp.transpose(w_uk[l], (0, 2, 1)).astype(BF16),
        "w_uv": w_uv[l].astype(BF16),
        "lam_q1": row(lam_q1), "lam_k1": row(lam_k1), "lam_q2": row(lam_q2), "lam_k2": row(lam_k2),
        "g_da_head": row(g_da_head),
        "g_da_col": g_da_head[l].reshape(-1, 1).astype(F32),
        "w_o": w_o[l].astype(BF16),
        "g_ffn": row(g_ffn),
        "w_up": w_up[l].astype(BF16),
        "w_conv": w_conv[l].astype(F32),
        "b_conv": row(b_conv),
        "w_down": w_down[l].astype(BF16),
    }


def _all_keys_visible(q_pos, k_pos):
    return bool(np.all((k_pos[None, :] // CHUNK) <= (q_pos[:, None] // CHUNK)))


def kernel(x_prompt, x_sample, cache_dk, cache_dv, cache_ckv, cache_krope, state_conv, g_attn, w_in, b_gate, lam_q1, lam_k1, lam_q2, lam_k2, g_da_head, g_cq, w_uq, g_ckv, w_uk, w_uv, w_o, g_ffn, w_up, w_conv, b_conv, w_down, g_final):
    depth = w_in.shape[0]
    B, S, _ = x_prompt.shape
    Bs, Ss, _ = x_sample.shape
    past = cache_dk.shape[2]
    assert _all_keys_visible(past + np.arange(Ss), np.arange(past + Ss))

    gfin = g_final.reshape(1, -1).astype(F32)
    pos_p = jnp.arange(S, dtype=jnp.int32)
    pos_s = past + jnp.arange(Ss, dtype=jnp.int32)
    tm_in = min(S, 256)
    tm_ffn = min(S, 512)
    zero_state = jnp.zeros((B, CONV_W - 1, 2 * D_FF), F32)
    cache_dk2 = cache_dk.reshape(depth, Bs, past * DA_HEADS, DA_V_DIM)
    cache_dv2 = cache_dv.reshape(depth, Bs, past * DA_HEADS, DA_V_DIM)

    yp, ys = x_prompt, x_sample
    state_p = state_s = None
    conv_p, conv_s = [], []
    for l in range(depth):
        lw = _layer_weights(l, g_attn, w_in, b_gate, lam_q1, lam_k1, lam_q2, lam_k2, g_da_head,
                            g_cq, w_uq, g_ckv, w_uk, w_uv, w_o, g_ffn, w_up, w_conv, b_conv, w_down)
        lam_init = 0.8 - 0.6 * math.exp(-0.3 * l)
        final = l == depth - 1

        state_p, (q_da, k_b, v_b, gates, q_m, k_m, v_m) = _in_projection(
            yp, pos_p, lw, l, depth, state_p, absorbed=False, tm=tm_in)
        o_da = _prompt_diff_attention(q_da, k_b, v_b, lw, lam_init)
        o_mla = _prompt_mla_attention(q_m, k_m, v_m)
        yp, cv = _merge_ffn(yp, o_da, o_mla, gates, zero_state, lw, gfin, final=final, tm=tm_ffn)
        conv_p.append(cv)

        state_s, (q_da, k_b, v_b, gates, q_abs) = _in_projection(
            ys, pos_s, lw, l, depth, state_s, absorbed=True, tm=Ss)
        o_da = _sample_diff_attention(q_da, cache_dk2, cache_dv2, k_b, v_b, l, lw, lam_init)
        o_mla = _sample_mla_attention(q_abs, cache_ckv, cache_krope, state_s[2], state_s[3], l, lw)
        ys, cv = _merge_ffn(ys, o_da, o_mla, gates, state_conv[l], lw, gfin, final=final, tm=Ss)
        conv_s.append(cv)

    dk_p, dv_p, ckv_p, kr_p = state_p
    dk_s, dv_s, ckv_s, kr_s = state_s
    heads = lambda a: a.reshape(a.shape[:2] + (-1, DA_HEADS, DA_V_DIM))
    return (yp, ys, heads(dk_p), heads(dv_p), ckv_p, kr_p, jnp.stack(conv_p),
            heads(dk_s), heads(dv_s), ckv_s, kr_s, jnp.stack(conv_s))
```

```python
import functools
import math

import numpy as np
import jax
import jax.numpy as jnp
from jax import lax
from jax.experimental import pallas as pl
from jax.experimental.pallas import tpu as pltpu

F32 = jnp.float32
BF16 = jnp.bfloat16

V7X_VMEM_BYTES = 64 * 1024 * 1024
VMEM_LIMIT_BYTES = V7X_VMEM_BYTES - 8 * 1024 * 1024

D_MODEL = 1024
CHUNK = 64
ROPE_THETA = 10000.0
EPS = 1e-6
DA_HEAD_DIM = 64
DA_V_DIM = 2 * DA_HEAD_DIM
DA_HEADS = D_MODEL // DA_V_DIM
V_HEAD_DIM = 128
MLA_HEADS = D_MODEL // V_HEAD_DIM
NOPE_DIM = 128
ROPE_DIM = 64
Q_LORA = 384
KV_LORA = 256
MLA_SCALE = (NOPE_DIM + ROPE_DIM) ** -0.5
DA_SCALE = DA_HEAD_DIM ** -0.5
D_FF = 2816
CONV_W = 3
HEADS = 8
assert DA_HEADS == HEADS and MLA_HEADS == HEADS
HEAD_W = 128
MLA_QK_W = 2 * HEAD_W
BF16_SUBLANES = 16
V_ROWS = HEAD_W + BF16_SUBLANES
FF_CHUNK = 256
N_FF_CHUNKS = D_FF // FF_CHUNK
assert N_FF_CHUNKS * FF_CHUNK == D_FF

LOG2E = math.log2(math.e)
MASK_VALUE = -1e30


def _cparams(n_grid):
    return pltpu.CompilerParams(
        dimension_semantics=("arbitrary",) * n_grid,
        vmem_limit_bytes=VMEM_LIMIT_BYTES,
    )


def _const_spec(shape):
    nd = len(shape)
    return pl.BlockSpec(shape, lambda *_: (0,) * nd, pipeline_mode=pl.Buffered(1))


def _dot(a, b):
    return jnp.dot(a, b, preferred_element_type=F32)


def _dot_nt(a, b):
    return lax.dot_general(a, b, (((1,), (1,)), ((), ())), preferred_element_type=F32)


def _rms(x, g):
    return x * lax.rsqrt(jnp.mean(x * x, axis=-1, keepdims=True) + EPS) * g


N_STATE = 4


def _inproj_kernel(*refs, absorbed, aliased, tm):
    n_in = 18 if absorbed else 19
    ins = refs[:n_in]
    outs = refs[n_in + (N_STATE if aliased else 0):]
    (x_ref, g_ref, wq_ref, wk_ref, wv_ref, wcq_ref, wckv_ref, wkr_ref, wg_ref, bg_ref,
     gcq_ref, gckv_ref, cos_ref, sa_ref, sb_ref, wuqn_ref, wuqr_ref) = ins[:17]
    kf_ref, vf_ref, ckv_ref, kr_ref = outs[:N_STATE]
    qda_ref, kb_ref, vb_ref, gate_ref, qm_ref = outs[N_STATE:N_STATE + 5]

    x = x_ref[0]
    h = _rms(x, g_ref[...]).astype(BF16)
    cos_t = cos_ref[...]
    sin_a = sa_ref[...]
    sin_b = sb_ref[...]

    def rope(t):
        return (t * cos_t + pltpu.roll(t, HEAD_W - 32, 1) * sin_a
                + pltpu.roll(t, 32, 1) * sin_b)

    def head(t, i):
        return t[:, i * HEAD_W:(i + 1) * HEAD_W]

    def put_v(ref, i, t):
        if absorbed:
            ref[0, i] = t.astype(BF16)
        else:
            ref[0, i, :HEAD_W, :] = t.T.astype(BF16)
            ref[0, i, HEAD_W:, :] = jnp.ones((V_ROWS - HEAD_W, tm), BF16)

    cq = _rms(_dot(h, wcq_ref[...]), gcq_ref[...]).astype(BF16)
    ckv = _rms(_dot(h, wckv_ref[...]), gckv_ref[...])
    ckv_ref[0, 0] = ckv
    ckv_b = ckv.astype(BF16)

    qn = _dot(cq, wuqn_ref[...])
    qr = _dot(cq, wuqr_ref[...])
    if absorbed:
        wukt_ref = ins[17]
        for i in range(HEADS):
            ql = _dot(head(qn, i).astype(BF16), wukt_ref[i])
            qm_ref[0, i, :, :KV_LORA] = (ql * (MLA_SCALE * LOG2E)).astype(BF16)
            qm_ref[0, i, :, KV_LORA:] = (rope(head(qr, i)) * (MLA_SCALE * LOG2E)).astype(BF16)
    else:
        wukf_ref, wuvf_ref = ins[17:19]
        km_ref, vm_ref = outs[N_STATE + 5:N_STATE + 7]
        kn = _dot(ckv_b, wukf_ref[...])
        vm = _dot(ckv_b, wuvf_ref[...])
        for i in range(HEADS):
            qm_ref[0, i, :, :HEAD_W] = (head(qn, i) * (MLA_SCALE * LOG2E)).astype(BF16)
            qm_ref[0, i, :, HEAD_W:] = (rope(head(qr, i)) * (MLA_SCALE * LOG2E)).astype(BF16)
            km_ref[0, i, :, :HEAD_W] = head(kn, i).astype(BF16)
            put_v(vm_ref, i, head(vm, i))

    zq = _dot(h, wq_ref[...])
    for i in range(HEADS):
        qda_ref[0, i] = (rope(head(zq, i)) * (DA_SCALE * LOG2E)).astype(BF16)

    zk = _dot(h, wk_ref[...])
    for i in range(HEADS):
        r = rope(head(zk, i))
        kf_ref[0, 0, pl.ds(i, tm, stride=HEADS), :] = r
        kb_ref[0, i] = r.astype(BF16)

    zv = _dot(h, wv_ref[...])
    for i in range(HEADS):
        vf_ref[0, 0, pl.ds(i, tm, stride=HEADS), :] = head(zv, i)
        put_v(vb_ref, i, head(zv, i))

    gate_ref[0] = jax.nn.sigmoid(_dot(h, wg_ref[...]) + bg_ref[...]).astype(BF16)

    kr = rope(_dot(h, wkr_ref[...]))
    kr_ref[0, 0] = kr[:, :ROPE_DIM]
    if not absorbed:
        kr_b = kr.astype(BF16)
        for i in range(HEADS):
            km_ref[0, i, :, HEAD_W:] = kr_b


def _rope_tables(pos):
    half = ROPE_DIM // 2
    inv = jnp.power(ROPE_THETA, -jnp.arange(half, dtype=F32) / half)
    ang = pos.astype(F32)[:, None] * inv[None, :]
    cos, sin = jnp.cos(ang), jnp.sin(ang)
    zero = jnp.zeros_like(sin)
    reps = HEAD_W // ROPE_DIM
    cos_t = jnp.tile(jnp.concatenate([cos, cos], axis=1), (1, reps))
    sin_a = jnp.tile(jnp.concatenate([-sin, zero], axis=1), (1, reps))
    sin_b = jnp.tile(jnp.concatenate([zero, sin], axis=1), (1, reps))
    return cos_t, sin_a, sin_b


def _in_projection(x, pos, lw, layer, depth, state, *, absorbed, tm):
    B, S, _ = x.shape
    assert S % tm == 0
    ns = S // tm
    cos_t, sin_a, sin_b = _rope_tables(pos)

    tok = lambda w: pl.BlockSpec((1, tm, w), lambda b, s: (b, s, 0))
    hd = lambda w: pl.BlockSpec((1, HEADS, tm, w), lambda b, s: (b, 0, s, 0))
    hdt = pl.BlockSpec((1, HEADS, V_ROWS, tm), lambda b, s: (b, 0, 0, s))
    vspec = hd(HEAD_W) if absorbed else hdt
    vshape = (B, HEADS, S, HEAD_W) if absorbed else (B, HEADS, V_ROWS, S)
    st = lambda rows, w: pl.BlockSpec((1, 1, rows, w), lambda b, s: (layer, b, s, 0))
    tab = pl.BlockSpec((tm, HEAD_W), lambda b, s: (s, 0))

    consts = [lw["g_attn"], lw["w_q"], lw["w_k"], lw["w_v"], lw["w_cq"], lw["w_ckv"], lw["w_kr"],
              lw["w_g"], lw["b_gate"], lw["g_cq"], lw["g_ckv"]]
    mla_w = [lw["w_uqn"], lw["w_uqr"]] + ([lw["w_ukt"]] if absorbed else [lw["w_ukf"], lw["w_uvf"]])
    ins = [x] + consts + [cos_t, sin_a, sin_b] + mla_w
    in_specs = ([tok(D_MODEL)] + [_const_spec(a.shape) for a in consts] + [tab, tab, tab]
                + [_const_spec(a.shape) for a in mla_w])
    aliases = {}
    if state is not None:
        for i, a in enumerate(state):
            aliases[len(ins)] = i
            ins.append(a)
            in_specs.append(pl.BlockSpec(memory_space=pl.ANY))

    sds = jax.ShapeDtypeStruct
    qm_w = KV_LORA + HEAD_W if absorbed else MLA_QK_W
    state_dims = ((S * HEADS, tm * HEADS, HEAD_W), (S * HEADS, tm * HEADS, HEAD_W),
                  (S, tm, KV_LORA), (S, tm, ROPE_DIM))
    out_shape = [sds((depth, B, rows, w), F32) for rows, _, w in state_dims] + [
        sds((B, HEADS, S, HEAD_W), BF16),
        sds((B, HEADS, S, HEAD_W), BF16),
        sds(vshape, BF16),
        sds((B, S, 2 * D_MODEL), BF16),
        sds((B, HEADS, S, qm_w), BF16),
    ]
    out_specs = [st(blk, w) for _, blk, w in state_dims] + [hd(HEAD_W), hd(HEAD_W), vspec,
                                                            tok(2 * D_MODEL), hd(qm_w)]
    if not absorbed:
        out_shape += [sds((B, HEADS, S, MLA_QK_W), BF16), sds(vshape, BF16)]
        out_specs += [hd(MLA_QK_W), vspec]

    outs = pl.pallas_call(
        functools.partial(_inproj_kernel, absorbed=absorbed, aliased=state is not None, tm=tm),
        grid=(B, ns),
        in_specs=in_specs,
        out_specs=out_specs,
        out_shape=out_shape,
        input_output_aliases=aliases,
        compiler_params=_cparams(2),
        name="in_projection_absorbed" if absorbed else "in_projection",
    )(*ins)
    return tuple(outs[:N_STATE]), tuple(outs[N_STATE:])


def _lambda(lq1_ref, lk1_ref, lq2_ref, lk2_ref, lam_init):
    a = jnp.sum(lq1_ref[...] * lk1_ref[...], axis=-1, keepdims=True)
    b = jnp.sum(lq2_ref[...] * lk2_ref[...], axis=-1, keepdims=True)
    return jnp.exp(a) - jnp.exp(b) + lam_init


def _diff_finish(o1, o2, lam, g, lam_init):
    o = o1 - lam * o2
    return _rms(o, g) * (1.0 - lam_init)


def _key_blocks(qi, tq, tk):
    n_full = (tq * qi) // tk
    blocks = [(tk * jj, tk) for jj in range(n_full)]
    blocks.append((tk * n_full, tq * (qi + 1) - tk * n_full))
    return blocks


def _prompt_attn_kernel(*refs, tq, tk, lookahead, n_q, n_heads, diff, lam_init):
    if diff:
        lq1_ref, lk1_ref, lq2_ref, lk2_ref, g_ref, q_ref, k_ref, vt_ref, o_ref = refs
    else:
        q_ref, k_ref, vt_ref, o_ref = refs
    n_br = 2 if diff else 1
    queries = {}
    masks = {}

    def branch_queries(hd, qi):
        if (hd, qi) not in queries:
            q = q_ref[0, hd, qi * tq:(qi + 1) * tq, :]
            if diff:
                lane = lax.broadcasted_iota(jnp.int32, q.shape, 1)
                zero = jnp.zeros_like(q)
                queries[hd, qi] = (jnp.where(lane < DA_HEAD_DIM, q, zero),
                                   jnp.where(lane >= DA_HEAD_DIM, q, zero))
            else:
                queries[hd, qi] = (q,)
        return queries[hd, qi]

    def chunk_mask(size):
        if size not in masks:
            kc = lax.broadcasted_iota(jnp.int32, (size, tq), 0) // CHUNK
            qc = (lax.broadcasted_iota(jnp.int32, (size, tq), 1) + (size - tq)) // CHUNK
            masks[size] = kc <= qc
        return masks[size]

    def scores(task):
        hd, qi, start, size, last, b = task
        s = _dot_nt(k_ref[0, hd, start:start + size, :], branch_queries(hd, qi)[b])
        return jnp.where(chunk_mask(size), s, MASK_VALUE) if last else s

    tasks = []
    for hd in range(n_heads):
        for qi in range(n_q):
            blocks = _key_blocks(qi, tq, tk)
            for start, size in blocks:
                for b in range(n_br):
                    tasks.append((hd, qi, start, size, (start, size) == blocks[-1], b))
    m = [None] * n_br
    acc = [None] * n_br
    pending = [scores(task) for task in tasks[:lookahead]]
    for n, (hd, qi, start, size, last, b) in enumerate(tasks):
        s = pending.pop(0)
        if n + lookahead < len(tasks):
            pending.append(scores(tasks[n + lookahead]))
        vt = vt_ref[0, hd, :, start:start + size]
        s_max = jnp.max(s, axis=0, keepdims=True)
        if start == 0:
            m[b] = s_max
            acc[b] = _dot(vt, jnp.exp2(s - s_max).astype(BF16))
        else:
            m_new = jnp.maximum(m[b], s_max)
            acc[b] = jnp.exp2(m[b] - m_new) * acc[b] + _dot(vt, jnp.exp2(s - m_new).astype(BF16))
            m[b] = m_new
        if last and b == n_br - 1:
            outs = [a[:HEAD_W] / a[HEAD_W:HEAD_W + 1] for a in acc]
            if diff:
                lam = _lambda(lq1_ref, lk1_ref, lq2_ref, lk2_ref, lam_init)
                o_t = outs[0] - lam * outs[1]
                o_t = (o_t * lax.rsqrt(jnp.mean(o_t * o_t, axis=0, keepdims=True) + EPS)
                       * g_ref[...] * (1.0 - lam_init))
            else:
                o_t = outs[0]
            o_ref[0, qi * tq:(qi + 1) * tq, hd * HEAD_W:(hd + 1) * HEAD_W] = o_t.T.astype(BF16)


ATTN_TQ = 256
ATTN_TK = 512
ATTN_LOOKAHEAD = 6
ATTN_HEADS_PER_STEP = 2


def _prompt_attention(q, k, vt, small, *, diff, lam_init, name):
    B, H, S, dq = q.shape
    tq, tk = min(S, ATTN_TQ), min(S, ATTN_TK)
    hps = ATTN_HEADS_PER_STEP
    assert S % tq == 0 and S % tk == 0 and tk % tq == 0 and tq % CHUNK == 0 and H % hps == 0
    qk = pl.BlockSpec((1, hps, S, dq), lambda b, h: (b, h, 0, 0))
    return pl.pallas_call(
        functools.partial(_prompt_attn_kernel, tq=tq, tk=tk, lookahead=ATTN_LOOKAHEAD,
                          n_q=S // tq, n_heads=hps, diff=diff, lam_init=lam_init),
        grid=(B, H // hps),
        in_specs=[_const_spec(a.shape) for a in small]
        + [qk, qk, pl.BlockSpec((1, hps, V_ROWS, S), lambda b, h: (b, h, 0, 0))],
        out_specs=pl.BlockSpec((1, S, hps * HEAD_W), lambda b, h: (b, 0, h)),
        out_shape=jax.ShapeDtypeStruct((B, S, D_MODEL), BF16),
        compiler_params=_cparams(2),
        name=name,
    )(*small, q, k, vt)


def _prompt_diff_attention(q, k, vt, lw, lam_init):
    small = [lw["lam_q1"], lw["lam_k1"], lw["lam_q2"], lw["lam_k2"], lw["g_da_col"]]
    return _prompt_attention(q, k, vt, small, diff=True, lam_init=lam_init, name="diff_attention")


def _prompt_mla_attention(q, k, vt):
    return _prompt_attention(q, k, vt, [], diff=False, lam_init=None, name="mla_attention")


def _two_part_softmax(s_c, s_n):
    m = jnp.maximum(jnp.max(s_c, axis=-1, keepdims=True), jnp.max(s_n, axis=-1, keepdims=True))
    p_c = jnp.exp2(s_c - m)
    p_n = jnp.exp2(s_n - m)
    l = jnp.sum(p_c, axis=-1, keepdims=True) + jnp.sum(p_n, axis=-1, keepdims=True)
    return p_c.astype(BF16), p_n.astype(BF16), l


def _diff_decode_kernel(lq1_ref, lk1_ref, lq2_ref, lk2_ref, g_ref, q_ref, kc_ref, vc_ref,
                        kn_ref, vn_ref, o_ref, *, lam_init, t_cache):
    lam = _lambda(lq1_ref, lk1_ref, lq2_ref, lk2_ref, lam_init)
    for h in range(HEADS):
        q = q_ref[0, h]
        lane = lax.broadcasted_iota(jnp.int32, q.shape, 1)
        zero = jnp.zeros_like(q)
        kc = kc_ref[0, 0, pl.ds(h, t_cache, stride=HEADS), :].astype(BF16)
        vc = vc_ref[0, 0, pl.ds(h, t_cache, stride=HEADS), :].astype(BF16)
        kn = kn_ref[0, h]
        vn = vn_ref[0, h]
        outs = []
        for qq in (jnp.where(lane < DA_HEAD_DIM, q, zero), jnp.where(lane >= DA_HEAD_DIM, q, zero)):
            p_c, p_n, l = _two_part_softmax(_dot_nt(qq, kc), _dot_nt(qq, kn))
            outs.append((_dot(p_c, vc) + _dot(p_n, vn)) / l)
        o_ref[0, :, h * HEAD_W:(h + 1) * HEAD_W] = _diff_finish(
            outs[0], outs[1], lam, g_ref[...], lam_init).astype(BF16)


def _sample_diff_attention(q, cache_k, cache_v, k_new, v_new, layer, lw, lam_init):
    B, H, S, _ = q.shape
    rows = cache_k.shape[2]
    small = [lw["lam_q1"], lw["lam_k1"], lw["lam_q2"], lw["lam_k2"], lw["g_da_head"]]
    new = pl.BlockSpec((1, H, S, HEAD_W), lambda b: (b, 0, 0, 0))
    cache = pl.BlockSpec((1, 1, rows, HEAD_W), lambda b: (layer, b, 0, 0))
    return pl.pallas_call(
        functools.partial(_diff_decode_kernel, lam_init=lam_init, t_cache=rows // H),
        grid=(B,),
        in_specs=[_const_spec(a.shape) for a in small] + [new, cache, cache, new, new],
        out_specs=pl.BlockSpec((1, S, D_MODEL), lambda b: (b, 0, 0)),
        out_shape=jax.ShapeDtypeStruct((B, S, D_MODEL), BF16),
        compiler_params=_cparams(1),
        name="diff_attention_decode",
    )(*small, q, cache_k, cache_v, k_new, v_new)


def _mla_decode_kernel(q_ref, cc_ref, rc_ref, cn_ref, rn_ref, wuv_ref, o_ref, *, s_len):
    q = q_ref[0]
    ql = q[:, :KV_LORA]
    qr = q[:, KV_LORA:KV_LORA + ROPE_DIM]
    cc = cc_ref[0, 0].astype(BF16)
    cn = cn_ref[0, 0].astype(BF16)
    s_c = _dot_nt(ql, cc) + _dot_nt(qr, rc_ref[0, 0].astype(BF16))
    s_n = _dot_nt(ql, cn) + _dot_nt(qr, rn_ref[0, 0].astype(BF16))
    p_c, p_n, l = _two_part_softmax(s_c, s_n)
    o_lat = ((_dot(p_c, cc) + _dot(p_n, cn)) / l).astype(BF16)
    for i in range(HEADS):
        o_ref[0, :, i * HEAD_W:(i + 1) * HEAD_W] = _dot(
            o_lat[i * s_len:(i + 1) * s_len], wuv_ref[i]).astype(BF16)


def _sample_mla_attention(q_abs, cache_ckv, cache_kr, new_ckv, new_kr, layer, lw):
    B, H, S, W = q_abs.shape
    q2 = q_abs.reshape(B, H * S, W)
    per_layer = lambda a: pl.BlockSpec((1, 1) + a.shape[2:], lambda b: (layer, b, 0, 0))
    return pl.pallas_call(
        functools.partial(_mla_decode_kernel, s_len=S),
        grid=(B,),
        in_specs=[pl.BlockSpec((1, H * S, W), lambda b: (b, 0, 0)),
                  per_layer(cache_ckv), per_layer(cache_kr), per_layer(new_ckv), per_layer(new_kr),
                  _const_spec(lw["w_uv"].shape)],
        out_specs=pl.BlockSpec((1, S, D_MODEL), lambda b: (b, 0, 0)),
        out_shape=jax.ShapeDtypeStruct((B, S, D_MODEL), BF16),
        compiler_params=_cparams(1),
        name="mla_attention_decode",
    )(q2, cache_ckv, cache_kr, new_ckv, new_kr, lw["w_uv"])


F32_SUBLANES = 8


def _merge_ffn_kernel(x_ref, oda_ref, omla_ref, gate_ref, prev_ref, wo_ref, gffn_ref, wup_ref,
                      wconv_ref, bconv_ref, wdown_ref, gfin_ref, y_ref, conv_ref,
                      carry, xbuf, act, *, tm, final):
    sub = F32_SUBLANES
    nv = tm // sub
    n_slab = D_MODEL // HEAD_W

    @pl.when(pl.program_id(1) == 0)
    def _():
        carry[...] = prev_ref[0]

    def to_slabs(val):
        for k in range(n_slab):
            xbuf[k] = val[:, k * HEAD_W:(k + 1) * HEAD_W]

    def from_slabs(row_tile):
        return jnp.concatenate(
            [jnp.concatenate([row_tile(k, c) for k in range(n_slab)], axis=1) for c in range(nv)],
            axis=0)

    gates = gate_ref[0]
    merged = gates[:, :D_MODEL] * oda_ref[0] + gates[:, D_MODEL:] * omla_ref[0]
    to_slabs(x_ref[0] + _dot(merged, wo_ref[...]))
    x1 = from_slabs(lambda k, c: xbuf[k, pl.ds(c, sub, stride=nv), :])
    h = _rms(x1, gffn_ref[...]).astype(BF16)
    first_row = lax.broadcasted_iota(jnp.int32, (sub, FF_CHUNK), 0) == 0

    def conv_cols(c0):
        cols = slice(c0, c0 + FF_CHUNK)
        u = _dot(h, wup_ref[:, cols])
        e1 = jnp.where(first_row, carry[1:2, cols], pltpu.roll(u[tm - sub:], 1, 0))
        e2 = jnp.where(first_row, carry[0:1, cols], pltpu.roll(u[tm - 2 * sub:tm - sub], 1, 0))
        prev1 = jnp.concatenate([e1, u[:tm - sub]], axis=0)
        prev2 = jnp.concatenate([e2, e1, u[:tm - 2 * sub]], axis=0)
        c = (bconv_ref[:, cols] + wconv_ref[0:1, cols] * prev2 + wconv_ref[1:2, cols] * prev1
             + wconv_ref[2:3, cols] * u)
        tail = jnp.concatenate([u[tm - sub - 1:tm - sub], u[tm - 1:tm]], axis=0)
        carry[:, cols] = tail
        conv_ref[0, :, cols] = tail
        return c

    for j in range(N_FF_CHUNKS):
        ca = conv_cols(j * FF_CHUNK)
        cb = conv_cols(D_FF + j * FF_CHUNK)
        act[:, j * FF_CHUNK:(j + 1) * FF_CHUNK] = (jax.nn.silu(ca) * cb).astype(BF16)

    x2 = x1 + _dot(act[...], wdown_ref[...])
    if final:
        x2 = _rms(x2, gfin_ref[...])
    to_slabs(x2)

    def natural_tile(k, tile):
        t0 = tile * sub
        if nv % sub == 0:
            return xbuf[k, pl.ds((t0 % nv) * sub + t0 // nv, sub, stride=sub), :]
        return jnp.concatenate(
            [xbuf[k, pl.ds(t0 // nv + g, nv, stride=sub), :] for g in range(sub // nv)], axis=0)

    y_ref[0] = from_slabs(natural_tile)


def _merge_ffn(x, o_da, o_mla, gates, prev_conv, lw, g_final, *, final, tm):
    B, S, _ = x.shape
    nv = tm // F32_SUBLANES
    assert S % tm == 0 and tm % BF16_SUBLANES == 0 and nv >= CONV_W - 1
    assert nv % F32_SUBLANES == 0 or F32_SUBLANES % nv == 0
    tok = lambda w: pl.BlockSpec((1, tm, w), lambda b, s: (b, s, 0))
    state = pl.BlockSpec((1, CONV_W - 1, 2 * D_FF), lambda b, s: (b, 0, 0))
    consts = [lw["w_o"], lw["g_ffn"], lw["w_up"], lw["w_conv"], lw["b_conv"], lw["w_down"], g_final]
    return pl.pallas_call(
        functools.partial(_merge_ffn_kernel, tm=tm, final=final),
        grid=(B, S // tm),
        in_specs=[tok(D_MODEL), tok(D_MODEL), tok(D_MODEL), tok(2 * D_MODEL), state]
        + [_const_spec(a.shape) for a in consts],
        out_specs=[tok(D_MODEL), state],
        out_shape=[jax.ShapeDtypeStruct((B, S, D_MODEL), F32),
                   jax.ShapeDtypeStruct((B, CONV_W - 1, 2 * D_FF), F32)],
        scratch_shapes=[pltpu.VMEM((CONV_W - 1, 2 * D_FF), F32),
                        pltpu.VMEM((D_MODEL // HEAD_W, tm, HEAD_W), F32),
                        pltpu.VMEM((tm, D_FF), BF16)],
        compiler_params=_cparams(2),
        name="merge_ffn_final" if final else "merge_ffn",
    )(x, o_da, o_mla, gates, prev_conv, *consts)


def _layer_weights(l, g_attn, w_in, b_gate, lam_q1, lam_k1, lam_q2, lam_k2, g_da_head, g_cq, w_uq,
                   g_ckv, w_uk, w_uv, w_o, g_ffn, w_up, w_conv, b_conv, w_down):
    row = lambda a: a[l].reshape(1, -1).astype(F32)
    w = w_in[l]
    o1 = D_MODEL
    o2 = o1 + D_MODEL
    o3 = o2 + D_MODEL
    o4 = o3 + Q_LORA
    o5 = o4 + KV_LORA
    o6 = o5 + ROPE_DIM
    uq = w_uq[l]
    pad_rope = lambda a: jnp.pad(a, [(0, 0)] * (a.ndim - 1) + [(0, HEAD_W - ROPE_DIM)])
    return {
        "g_attn": row(g_attn),
        "w_q": w[:, :o1].astype(BF16),
        "w_k": w[:, o1:o2].astype(BF16),
        "w_v": w[:, o2:o3].astype(BF16),
        "w_cq": w[:, o3:o4].astype(BF16),
        "w_ckv": w[:, o4:o5].astype(BF16),
        "w_kr": pad_rope(w[:, o5:o6]).astype(BF16),
        "w_g": w[:, o6:].astype(BF16),
        "b_gate": row(b_gate),
        "g_cq": row(g_cq),
        "g_ckv": row(g_ckv),
        "w_uqn": uq[:, :, :NOPE_DIM].reshape(Q_LORA, HEADS * NOPE_DIM).astype(BF16),
        "w_uqr": pad_rope(uq[:, :, NOPE_DIM:]).reshape(Q_LORA, HEADS * HEAD_W).astype(BF16),
        "w_ukf": jnp.transpose(w_uk[l], (1, 0, 2)).reshape(KV_LORA, HEADS * NOPE_DIM).astype(BF16),
        "w_uvf": jnp.transpose(w_uv[l], (1, 0, 2)).reshape(KV_LORA, HEADS * V_HEAD_DIM).astype(BF16),
        "w_ukt": jnp.transpose(w_uk[l], (0, 2, 1)).astype(BF16),
        "w_uv": w_uv[l].astype(BF16),
        "lam_q1": row(lam_q1), "lam_k1": row(lam_k1), "lam_q2": row(lam_q2), "lam_k2": row(lam_k2),
        "g_da_head": row(g_da_head),
        "g_da_col": g_da_head[l].reshape(-1, 1).astype(F32),
        "w_o": w_o[l].astype(BF16),
        "g_ffn": row(g_ffn),
        "w_up": w_up[l].astype(BF16),
        "w_conv": w_conv[l].astype(F32),
        "b_conv": row(b_conv),
        "w_down": w_down[l].astype(BF16),
    }


def _all_keys_visible(q_pos, k_pos):
    return bool(np.all((k_pos[None, :] // CHUNK) <= (q_pos[:, None] // CHUNK)))


def kernel(x_prompt, x_sample, cache_dk, cache_dv, cache_ckv, cache_krope, state_conv, g_attn, w_in, b_gate, lam_q1, lam_k1, lam_q2, lam_k2, g_da_head, g_cq, w_uq, g_ckv, w_uk, w_uv, w_o, g_ffn, w_up, w_conv, b_conv, w_down, g_final):
    depth = w_in.shape[0]
    B, S, _ = x_prompt.shape
    Bs, Ss, _ = x_sample.shape
    past = cache_dk.shape[2]
    assert _all_keys_visible(past + np.arange(Ss), np.arange(past + Ss))

    gfin = g_final.reshape(1, -1).astype(F32)
    pos_p = jnp.arange(S, dtype=jnp.int32)
    pos_s = past + jnp.arange(Ss, dtype=jnp.int32)
    tm_in = min(S, 256)
    tm_ffn = min(S, 512)
    zero_state = jnp.zeros((B, CONV_W - 1, 2 * D_FF), F32)
    cache_dk2 = cache_dk.reshape(depth, Bs, past * DA_HEADS, DA_V_DIM)
    cache_dv2 = cache_dv.reshape(depth, Bs, past * DA_HEADS, DA_V_DIM)

    yp, ys = x_prompt, x_sample
    state_p = state_s = None
    conv_p, conv_s = [], []
    for l in range(depth):
        lw = _layer_weights(l, g_attn, w_in, b_gate, lam_q1, lam_k1, lam_q2, lam_k2, g_da_head,
                            g_cq, w_uq, g_ckv, w_uk, w_uv, w_o, g_ffn, w_up, w_conv, b_conv, w_down)
        lam_init = 0.8 - 0.6 * math.exp(-0.3 * l)
        final = l == depth - 1

        state_p, (q_da, k_b, v_b, gates, q_m, k_m, v_m) = _in_projection(
            yp, pos_p, lw, l, depth, state_p, absorbed=False, tm=tm_in)
        o_da = _prompt_diff_attention(q_da, k_b, v_b, lw, lam_init)
        o_mla = _prompt_mla_attention(q_m, k_m, v_m)
        yp, cv = _merge_ffn(yp, o_da, o_mla, gates, zero_state, lw, gfin, final=final, tm=tm_ffn)
        conv_p.append(cv)

        state_s, (q_da, k_b, v_b, gates, q_abs) = _in_projection(
            ys, pos_s, lw, l, depth, state_s, absorbed=True, tm=Ss)
        o_da = _sample_diff_attention(q_da, cache_dk2, cache_dv2, k_b, v_b, l, lw, lam_init)
        o_mla = _sample_mla_attention(q_abs, cache_ckv, cache_krope, state_s[2], state_s[3], l, lw)
        ys, cv = _merge_ffn(ys, o_da, o_mla, gates, state_conv[l], lw, gfin, final=final, tm=Ss)
        conv_s.append(cv)

    dk_p, dv_p, ckv_p, kr_p = state_p
    dk_s, dv_s, ckv_s, kr_s = state_s
    heads = lambda a: a.reshape(a.shape[:2] + (-1, DA_HEADS, DA_V_DIM))
    return (yp, ys, heads(dk_p), heads(dv_p), ckv_p, kr_p, jnp.stack(conv_p),
            heads(dk_s), heads(dv_s), ckv_s, kr_s, jnp.stack(conv_s))
```

```python
import functools
import math

import numpy as np
import jax
import jax.numpy as jnp
from jax import lax
from jax.experimental import pallas as pl
from jax.experimental.pallas import tpu as pltpu

F32 = jnp.float32
BF16 = jnp.bfloat16

V7X_VMEM_BYTES = 64 * 1024 * 1024
VMEM_LIMIT_BYTES = V7X_VMEM_BYTES - 8 * 1024 * 1024

D_MODEL = 1024
CHUNK = 64
ROPE_THETA = 10000.0
EPS = 1e-6
DA_HEAD_DIM = 64
DA_V_DIM = 2 * DA_HEAD_DIM
DA_HEADS = D_MODEL // DA_V_DIM
V_HEAD_DIM = 128
MLA_HEADS = D_MODEL // V_HEAD_DIM
NOPE_DIM = 128
ROPE_DIM = 64
Q_LORA = 384
KV_LORA = 256
MLA_SCALE = (NOPE_DIM + ROPE_DIM) ** -0.5
DA_SCALE = DA_HEAD_DIM ** -0.5
D_FF = 2816
CONV_W = 3
HEADS = 8
assert DA_HEADS == HEADS and MLA_HEADS == HEADS
HEAD_W = 128
MLA_QK_W = 2 * HEAD_W
BF16_SUBLANES = 16
V_ROWS = HEAD_W + BF16_SUBLANES
FF_CHUNK = 256
N_FF_CHUNKS = D_FF // FF_CHUNK
assert N_FF_CHUNKS * FF_CHUNK == D_FF

LOG2E = math.log2(math.e)
MASK_VALUE = -1e30


def _cparams(n_grid):
    return pltpu.CompilerParams(
        dimension_semantics=("arbitrary",) * n_grid,
        vmem_limit_bytes=VMEM_LIMIT_BYTES,
    )


def _const_spec(shape):
    nd = len(shape)
    return pl.BlockSpec(shape, lambda *_: (0,) * nd, pipeline_mode=pl.Buffered(1))


def _dot(a, b):
    return jnp.dot(a, b, preferred_element_type=F32)


def _dot_nt(a, b):
    return lax.dot_general(a, b, (((1,), (1,)), ((), ())), preferred_element_type=F32)


def _rms(x, g):
    return x * lax.rsqrt(jnp.mean(x * x, axis=-1, keepdims=True) + EPS) * g


N_STATE = 4


def _inproj_kernel(*refs, absorbed, aliased, tm):
    n_in = 18 if absorbed else 19
    ins = refs[:n_in]
    outs = refs[n_in + (N_STATE if aliased else 0):]
    (x_ref, g_ref, wq_ref, wk_ref, wv_ref, wcq_ref, wckv_ref, wkr_ref, wg_ref, bg_ref,
     gcq_ref, gckv_ref, cos_ref, sa_ref, sb_ref, wuqn_ref, wuqr_ref) = ins[:17]
    kf_ref, vf_ref, ckv_ref, kr_ref = outs[:N_STATE]
    qda_ref, kb_ref, vb_ref, gate_ref, qm_ref = outs[N_STATE:N_STATE + 5]

    x = x_ref[0]
    h = _rms(x, g_ref[...]).astype(BF16)
    cos_t = cos_ref[...]
    sin_a = sa_ref[...]
    sin_b = sb_ref[...]

    def rope(t):
        return (t * cos_t + pltpu.roll(t, HEAD_W - 32, 1) * sin_a
                + pltpu.roll(t, 32, 1) * sin_b)

    def head(t, i):
        return t[:, i * HEAD_W:(i + 1) * HEAD_W]

    def put_v(ref, i, t):
        if absorbed:
            ref[0, i] = t.astype(BF16)
        else:
            ref[0, i, :HEAD_W, :] = t.T.astype(BF16)
            ref[0, i, HEAD_W:, :] = jnp.ones((V_ROWS - HEAD_W, tm), BF16)

    cq = _rms(_dot(h, wcq_ref[...]), gcq_ref[...]).astype(BF16)
    ckv = _rms(_dot(h, wckv_ref[...]), gckv_ref[...])
    ckv_ref[0, 0] = ckv
    ckv_b = ckv.astype(BF16)

    qn = _dot(cq, wuqn_ref[...])
    qr = _dot(cq, wuqr_ref[...])
    if absorbed:
        wukt_ref = ins[17]
        for i in range(HEADS):
            ql = _dot(head(qn, i).astype(BF16), wukt_ref[i])
            qm_ref[0, i, :, :KV_LORA] = (ql * (MLA_SCALE * LOG2E)).astype(BF16)
            qm_ref[0, i, :, KV_LORA:] = (rope(head(qr, i)) * (MLA_SCALE * LOG2E)).astype(BF16)
    else:
        wukf_ref, wuvf_ref = ins[17:19]
        km_ref, vm_ref = outs[N_STATE + 5:N_STATE + 7]
        kn = _dot(ckv_b, wukf_ref[...])
        vm = _dot(ckv_b, wuvf_ref[...])
        for i in range(HEADS):
            qm_ref[0, i, :, :HEAD_W] = (head(qn, i) * (MLA_SCALE * LOG2E)).astype(BF16)
            qm_ref[0, i, :, HEAD_W:] = (rope(head(qr, i)) * (MLA_SCALE * LOG2E)).astype(BF16)
            km_ref[0, i, :, :HEAD_W] = head(kn, i).astype(BF16)
            put_v(vm_ref, i, head(vm, i))

    zq = _dot(h, wq_ref[...])
    for i in range(HEADS):
        qda_ref[0, i] = (rope(head(zq, i)) * (DA_SCALE * LOG2E)).astype(BF16)

    zk = _dot(h, wk_ref[...])
    for i in range(HEADS):
        r = rope(head(zk, i))
        kf_ref[0, 0, pl.ds(i, tm, stride=HEADS), :] = r
        kb_ref[0, i] = r.astype(BF16)

    zv = _dot(h, wv_ref[...])
    for i in range(HEADS):
        vf_ref[0, 0, pl.ds(i, tm, stride=HEADS), :] = head(zv, i)
        put_v(vb_ref, i, head(zv, i))

    gate_ref[0] = jax.nn.sigmoid(_dot(h, wg_ref[...]) + bg_ref[...]).astype(BF16)

    kr = rope(_dot(h, wkr_ref[...]))
    kr_ref[0, 0] = kr[:, :ROPE_DIM]
    if not absorbed:
        kr_b = kr.astype(BF16)
        for i in range(HEADS):
            km_ref[0, i, :, HEAD_W:] = kr_b


def _rope_tables(pos):
    half = ROPE_DIM // 2
    inv = jnp.power(ROPE_THETA, -jnp.arange(half, dtype=F32) / half)
    ang = pos.astype(F32)[:, None] * inv[None, :]
    cos, sin = jnp.cos(ang), jnp.sin(ang)
    zero = jnp.zeros_like(sin)
    reps = HEAD_W // ROPE_DIM
    cos_t = jnp.tile(jnp.concatenate([cos, cos], axis=1), (1, reps))
    sin_a = jnp.tile(jnp.concatenate([-sin, zero], axis=1), (1, reps))
    sin_b = jnp.tile(jnp.concatenate([zero, sin], axis=1), (1, reps))
    return cos_t, sin_a, sin_b


def _in_projection(x, pos, lw, layer, depth, state, *, absorbed, tm):
    B, S, _ = x.shape
    assert S % tm == 0
    ns = S // tm
    cos_t, sin_a, sin_b = _rope_tables(pos)

    tok = lambda w: pl.BlockSpec((1, tm, w), lambda b, s: (b, s, 0))
    hd = lambda w: pl.BlockSpec((1, HEADS, tm, w), lambda b, s: (b, 0, s, 0))
    hdt = pl.BlockSpec((1, HEADS, V_ROWS, tm), lambda b, s: (b, 0, 0, s))
    vspec = hd(HEAD_W) if absorbed else hdt
    vshape = (B, HEADS, S, HEAD_W) if absorbed else (B, HEADS, V_ROWS, S)
    st = lambda rows, w: pl.BlockSpec((1, 1, rows, w), lambda b, s: (layer, b, s, 0))
    tab = pl.BlockSpec((tm, HEAD_W), lambda b, s: (s, 0))

    consts = [lw["g_attn"], lw["w_q"], lw["w_k"], lw["w_v"], lw["w_cq"], lw["w_ckv"], lw["w_kr"],
              lw["w_g"], lw["b_gate"], lw["g_cq"], lw["g_ckv"]]
    mla_w = [lw["w_uqn"], lw["w_uqr"]] + ([lw["w_ukt"]] if absorbed else [lw["w_ukf"], lw["w_uvf"]])
    ins = [x] + consts + [cos_t, sin_a, sin_b] + mla_w
    in_specs = ([tok(D_MODEL)] + [_const_spec(a.shape) for a in consts] + [tab, tab, tab]
                + [_const_spec(a.shape) for a in mla_w])
    aliases = {}
    if state is not None:
        for i, a in enumerate(state):
            aliases[len(ins)] = i
            ins.append(a)
            in_specs.append(pl.BlockSpec(memory_space=pl.ANY))

    sds = jax.ShapeDtypeStruct
    qm_w = KV_LORA + HEAD_W if absorbed else MLA_QK_W
    state_dims = ((S * HEADS, tm * HEADS, HEAD_W), (S * HEADS, tm * HEADS, HEAD_W),
                  (S, tm, KV_LORA), (S, tm, ROPE_DIM))
    out_shape = [sds((depth, B, rows, w), F32) for rows, _, w in state_dims] + [
        sds((B, HEADS, S, HEAD_W), BF16),
        sds((B, HEADS, S, HEAD_W), BF16),
        sds(vshape, BF16),
        sds((B, S, 2 * D_MODEL), BF16),
        sds((B, HEADS, S, qm_w), BF16),
    ]
    out_specs = [st(blk, w) for _, blk, w in state_dims] + [hd(HEAD_W), hd(HEAD_W), vspec,
                                                            tok(2 * D_MODEL), hd(qm_w)]
    if not absorbed:
        out_shape += [sds((B, HEADS, S, MLA_QK_W), BF16), sds(vshape, BF16)]
        out_specs += [hd(MLA_QK_W), vspec]

    outs = pl.pallas_call(
        functools.partial(_inproj_kernel, absorbed=absorbed, aliased=state is not None, tm=tm),
        grid=(B, ns),
        in_specs=in_specs,
        out_specs=out_specs,
        out_shape=out_shape,
        input_output_aliases=aliases,
        compiler_params=_cparams(2),
        name="in_projection_absorbed" if absorbed else "in_projection",
    )(*ins)
    return tuple(outs[:N_STATE]), tuple(outs[N_STATE:])


def _lambda(lq1_ref, lk1_ref, lq2_ref, lk2_ref, lam_init):
    a = jnp.sum(lq1_ref[...] * lk1_ref[...], axis=-1, keepdims=True)
    b = jnp.sum(lq2_ref[...] * lk2_ref[...], axis=-1, keepdims=True)
    return jnp.exp(a) - jnp.exp(b) + lam_init


def _diff_finish(o1, o2, lam, g, lam_init):
    o = o1 - lam * o2
    return _rms(o, g) * (1.0 - lam_init)


def _key_blocks(qi, tq, tk):
    n_full = (tq * qi) // tk
    blocks = [(tk * jj, tk) for jj in range(n_full)]
    blocks.append((tk * n_full, tq * (qi + 1) - tk * n_full))
    return blocks


def _prompt_attn_kernel(*refs, tq, tk, lookahead, n_q, n_heads, diff, lam_init):
    if diff:
        lq1_ref, lk1_ref, lq2_ref, lk2_ref, g_ref, q_ref, k_ref, vt_ref, o_ref = refs
    else:
        q_ref, k_ref, vt_ref, o_ref = refs
    n_br = 2 if diff else 1
    queries = {}
    masks = {}

    def branch_queries(hd, qi):
        if (hd, qi) not in queries:
            q = q_ref[0, hd, qi * tq:(qi + 1) * tq, :]
            if diff:
                lane = lax.broadcasted_iota(jnp.int32, q.shape, 1)
                zero = jnp.zeros_like(q)
                queries[hd, qi] = (jnp.where(lane < DA_HEAD_DIM, q, zero),
                                   jnp.where(lane >= DA_HEAD_DIM, q, zero))
            else:
                queries[hd, qi] = (q,)
        return queries[hd, qi]

    def diag_mask():
        if not masks:
            kc = lax.broadcasted_iota(jnp.int32, (tq, tq), 0) // CHUNK
            qc = lax.broadcasted_iota(jnp.int32, (tq, tq), 1) // CHUNK
            masks[tq] = kc <= qc
        return masks[tq]

    def scores(task):
        hd, qi, start, size, last, b = task
        s = _dot_nt(k_ref[0, hd, start:start + size, :], branch_queries(hd, qi)[b])
        if not last:
            return s
        tail = jnp.where(diag_mask(), s[size - tq:], MASK_VALUE)
        return tail if size == tq else jnp.concatenate([s[:size - tq], tail], axis=0)

    tasks = []
    for hd in range(n_heads):
        for qi in range(n_q):
            blocks = _key_blocks(qi, tq, tk)
            for start, size in blocks:
                for b in range(n_br):
                    tasks.append((hd, qi, start, size, (start, size) == blocks[-1], b))
    m = [None] * n_br
    acc = [None] * n_br
    pending = [scores(task) for task in tasks[:lookahead]]
    for n, (hd, qi, start, size, last, b) in enumerate(tasks):
        s = pending.pop(0)
        if n + lookahead < len(tasks):
            pending.append(scores(tasks[n + lookahead]))
        vt = vt_ref[0, hd, :, start:start + size]
        s_max = jnp.max(s, axis=0, keepdims=True)
        if start == 0:
            m[b] = s_max
            acc[b] = _dot(vt, jnp.exp2(s - s_max).astype(BF16))
        else:
            m_new = jnp.maximum(m[b], s_max)
            acc[b] = jnp.exp2(m[b] - m_new) * acc[b] + _dot(vt, jnp.exp2(s - m_new).astype(BF16))
            m[b] = m_new
        if last and b == n_br - 1:
            outs = [a[:HEAD_W] / a[HEAD_W:HEAD_W + 1] for a in acc]
            if diff:
                lam = _lambda(lq1_ref, lk1_ref, lq2_ref, lk2_ref, lam_init)
                o_t = outs[0] - lam * outs[1]
                o_t = (o_t * lax.rsqrt(jnp.mean(o_t * o_t, axis=0, keepdims=True) + EPS)
                       * g_ref[...] * (1.0 - lam_init))
            else:
                o_t = outs[0]
            o_ref[0, qi * tq:(qi + 1) * tq, hd * HEAD_W:(hd + 1) * HEAD_W] = o_t.T.astype(BF16)


ATTN_TQ = 256
ATTN_TK = 512
ATTN_LOOKAHEAD = 6
ATTN_HEADS_PER_STEP = 4


def _prompt_attention(q, k, vt, small, *, diff, lam_init, name):
    B, H, S, dq = q.shape
    tq, tk = min(S, ATTN_TQ), min(S, ATTN_TK)
    hps = ATTN_HEADS_PER_STEP
    assert S % tq == 0 and S % tk == 0 and tk % tq == 0 and tq % CHUNK == 0 and H % hps == 0
    qk = pl.BlockSpec((1, hps, S, dq), lambda b, h: (b, h, 0, 0))
    return pl.pallas_call(
        functools.partial(_prompt_attn_kernel, tq=tq, tk=tk, lookahead=ATTN_LOOKAHEAD,
                          n_q=S // tq, n_heads=hps, diff=diff, lam_init=lam_init),
        grid=(B, H // hps),
        in_specs=[_const_spec(a.shape) for a in small]
        + [qk, qk, pl.BlockSpec((1, hps, V_ROWS, S), lambda b, h: (b, h, 0, 0))],
        out_specs=pl.BlockSpec((1, S, hps * HEAD_W), lambda b, h: (b, 0, h)),
        out_shape=jax.ShapeDtypeStruct((B, S, D_MODEL), BF16),
        compiler_params=_cparams(2),
        name=name,
    )(*small, q, k, vt)


def _prompt_diff_attention(q, k, vt, lw, lam_init):
    small = [lw["lam_q1"], lw["lam_k1"], lw["lam_q2"], lw["lam_k2"], lw["g_da_col"]]
    return _prompt_attention(q, k, vt, small, diff=True, lam_init=lam_init, name="diff_attention")


def _prompt_mla_attention(q, k, vt):
    return _prompt_attention(q, k, vt, [], diff=False, lam_init=None, name="mla_attention")


def _two_part_softmax(s_c, s_n):
    m = jnp.maximum(jnp.max(s_c, axis=-1, keepdims=True), jnp.max(s_n, axis=-1, keepdims=True))
    p_c = jnp.exp2(s_c - m)
    p_n = jnp.exp2(s_n - m)
    l = jnp.sum(p_c, axis=-1, keepdims=True) + jnp.sum(p_n, axis=-1, keepdims=True)
    return p_c.astype(BF16), p_n.astype(BF16), l


def _diff_decode_kernel(lq1_ref, lk1_ref, lq2_ref, lk2_ref, g_ref, q_ref, kc_ref, vc_ref,
                        kn_ref, vn_ref, o_ref, *, lam_init, t_cache):
    lam = _lambda(lq1_ref, lk1_ref, lq2_ref, lk2_ref, lam_init)
    for h in range(HEADS):
        q = q_ref[0, h]
        lane = lax.broadcasted_iota(jnp.int32, q.shape, 1)
        zero = jnp.zeros_like(q)
        kc = kc_ref[0, 0, pl.ds(h, t_cache, stride=HEADS), :].astype(BF16)
        vc = vc_ref[0, 0, pl.ds(h, t_cache, stride=HEADS), :].astype(BF16)
        kn = kn_ref[0, h]
        vn = vn_ref[0, h]
        outs = []
        for qq in (jnp.where(lane < DA_HEAD_DIM, q, zero), jnp.where(lane >= DA_HEAD_DIM, q, zero)):
            p_c, p_n, l = _two_part_softmax(_dot_nt(qq, kc), _dot_nt(qq, kn))
            outs.append((_dot(p_c, vc) + _dot(p_n, vn)) / l)
        o_ref[0, :, h * HEAD_W:(h + 1) * HEAD_W] = _diff_finish(
            outs[0], outs[1], lam, g_ref[...], lam_init).astype(BF16)


def _sample_diff_attention(q, cache_k, cache_v, k_new, v_new, layer, lw, lam_init):
    B, H, S, _ = q.shape
    rows = cache_k.shape[2]
    small = [lw["lam_q1"], lw["lam_k1"], lw["lam_q2"], lw["lam_k2"], lw["g_da_head"]]
    new = pl.BlockSpec((1, H, S, HEAD_W), lambda b: (b, 0, 0, 0))
    cache = pl.BlockSpec((1, 1, rows, HEAD_W), lambda b: (layer, b, 0, 0))
    return pl.pallas_call(
        functools.partial(_diff_decode_kernel, lam_init=lam_init, t_cache=rows // H),
        grid=(B,),
        in_specs=[_const_spec(a.shape) for a in small] + [new, cache, cache, new, new],
        out_specs=pl.BlockSpec((1, S, D_MODEL), lambda b: (b, 0, 0)),
        out_shape=jax.ShapeDtypeStruct((B, S, D_MODEL), BF16),
        compiler_params=_cparams(1),
        name="diff_attention_decode",
    )(*small, q, cache_k, cache_v, k_new, v_new)


def _mla_decode_kernel(q_ref, cc_ref, rc_ref, cn_ref, rn_ref, wuv_ref, o_ref, *, s_len):
    q = q_ref[0]
    ql = q[:, :KV_LORA]
    qr = q[:, KV_LORA:KV_LORA + ROPE_DIM]
    cc = cc_ref[0, 0].astype(BF16)
    cn = cn_ref[0, 0].astype(BF16)
    s_c = _dot_nt(ql, cc) + _dot_nt(qr, rc_ref[0, 0].astype(BF16))
    s_n = _dot_nt(ql, cn) + _dot_nt(qr, rn_ref[0, 0].astype(BF16))
    p_c, p_n, l = _two_part_softmax(s_c, s_n)
    o_lat = ((_dot(p_c, cc) + _dot(p_n, cn)) / l).astype(BF16)
    for i in range(HEADS):
        o_ref[0, :, i * HEAD_W:(i + 1) * HEAD_W] = _dot(
            o_lat[i * s_len:(i + 1) * s_len], wuv_ref[i]).astype(BF16)


def _sample_mla_attention(q_abs, cache_ckv, cache_kr, new_ckv, new_kr, layer, lw):
    B, H, S, W = q_abs.shape
    q2 = q_abs.reshape(B, H * S, W)
    per_layer = lambda a: pl.BlockSpec((1, 1) + a.shape[2:], lambda b: (layer, b, 0, 0))
    return pl.pallas_call(
        functools.partial(_mla_decode_kernel, s_len=S),
        grid=(B,),
        in_specs=[pl.BlockSpec((1, H * S, W), lambda b: (b, 0, 0)),
                  per_layer(cache_ckv), per_layer(cache_kr), per_layer(new_ckv), per_layer(new_kr),
                  _const_spec(lw["w_uv"].shape)],
        out_specs=pl.BlockSpec((1, S, D_MODEL), lambda b: (b, 0, 0)),
        out_shape=jax.ShapeDtypeStruct((B, S, D_MODEL), BF16),
        compiler_params=_cparams(1),
        name="mla_attention_decode",
    )(q2, cache_ckv, cache_kr, new_ckv, new_kr, lw["w_uv"])


F32_SUBLANES = 8


def _merge_ffn_kernel(x_ref, oda_ref, omla_ref, gate_ref, prev_ref, wo_ref, gffn_ref, wup_ref,
                      wconv_ref, bconv_ref, wdown_ref, gfin_ref, y_ref, conv_ref,
                      carry, xbuf, act, *, tm, final):
    sub = F32_SUBLANES
    nv = tm // sub
    n_slab = D_MODEL // HEAD_W

    @pl.when(pl.program_id(1) == 0)
    def _():
        carry[...] = prev_ref[0]

    def to_slabs(val):
        for k in range(n_slab):
            xbuf[k] = val[:, k * HEAD_W:(k + 1) * HEAD_W]

    def from_slabs(row_tile):
        return jnp.concatenate(
            [jnp.concatenate([row_tile(k, c) for k in range(n_slab)], axis=1) for c in range(nv)],
            axis=0)

    gates = gate_ref[0]
    merged = gates[:, :D_MODEL] * oda_ref[0] + gates[:, D_MODEL:] * omla_ref[0]
    to_slabs(x_ref[0] + _dot(merged, wo_ref[...]))
    x1 = from_slabs(lambda k, c: xbuf[k, pl.ds(c, sub, stride=nv), :])
    h = _rms(x1, gffn_ref[...]).astype(BF16)
    first_row = lax.broadcasted_iota(jnp.int32, (sub, FF_CHUNK), 0) == 0

    def conv_cols(c0):
        cols = slice(c0, c0 + FF_CHUNK)
        u = _dot(h, wup_ref[:, cols])
        e1 = jnp.where(first_row, carry[1:2, cols], pltpu.roll(u[tm - sub:], 1, 0))
        e2 = jnp.where(first_row, carry[0:1, cols], pltpu.roll(u[tm - 2 * sub:tm - sub], 1, 0))
        prev1 = jnp.concatenate([e1, u[:tm - sub]], axis=0)
        prev2 = jnp.concatenate([e2, e1, u[:tm - 2 * sub]], axis=0)
        c = (bconv_ref[:, cols] + wconv_ref[0:1, cols] * prev2 + wconv_ref[1:2, cols] * prev1
             + wconv_ref[2:3, cols] * u)
        tail = jnp.concatenate([u[tm - sub - 1:tm - sub], u[tm - 1:tm]], axis=0)
        carry[:, cols] = tail
        conv_ref[0, :, cols] = tail
        return c

    for j in range(N_FF_CHUNKS):
        ca = conv_cols(j * FF_CHUNK)
        cb = conv_cols(D_FF + j * FF_CHUNK)
        act[:, j * FF_CHUNK:(j + 1) * FF_CHUNK] = (jax.nn.silu(ca) * cb).astype(BF16)

    x2 = x1 + _dot(act[...], wdown_ref[...])
    if final:
        x2 = _rms(x2, gfin_ref[...])
    to_slabs(x2)

    def natural_tile(k, tile):
        t0 = tile * sub
        if nv % sub == 0:
            return xbuf[k, pl.ds((t0 % nv) * sub + t0 // nv, sub, stride=sub), :]
        return jnp.concatenate(
            [xbuf[k, pl.ds(t0 // nv + g, nv, stride=sub), :] for g in range(sub // nv)], axis=0)

    y_ref[0] = from_slabs(natural_tile)


def _merge_ffn(x, o_da, o_mla, gates, prev_conv, lw, g_final, *, final, tm):
    B, S, _ = x.shape
    nv = tm // F32_SUBLANES
    assert S % tm == 0 and tm % BF16_SUBLANES == 0 and nv >= CONV_W - 1
    assert nv % F32_SUBLANES == 0 or F32_SUBLANES % nv == 0
    tok = lambda w: pl.BlockSpec((1, tm, w), lambda b, s: (b, s, 0))
    state = pl.BlockSpec((1, CONV_W - 1, 2 * D_FF), lambda b, s: (b, 0, 0))
    consts = [lw["w_o"], lw["g_ffn"], lw["w_up"], lw["w_conv"], lw["b_conv"], lw["w_down"], g_final]
    return pl.pallas_call(
        functools.partial(_merge_ffn_kernel, tm=tm, final=final),
        grid=(B, S // tm),
        in_specs=[tok(D_MODEL), tok(D_MODEL), tok(D_MODEL), tok(2 * D_MODEL), state]
        + [_const_spec(a.shape) for a in consts],
        out_specs=[tok(D_MODEL), state],
        out_shape=[jax.ShapeDtypeStruct((B, S, D_MODEL), F32),
                   jax.ShapeDtypeStruct((B, CONV_W - 1, 2 * D_FF), F32)],
        scratch_shapes=[pltpu.VMEM((CONV_W - 1, 2 * D_FF), F32),
                        pltpu.VMEM((D_MODEL // HEAD_W, tm, HEAD_W), F32),
                        pltpu.VMEM((tm, D_FF), BF16)],
        compiler_params=_cparams(2),
        name="merge_ffn_final" if final else "merge_ffn",
    )(x, o_da, o_mla, gates, prev_conv, *consts)


def _layer_weights(l, g_attn, w_in, b_gate, lam_q1, lam_k1, lam_q2, lam_k2, g_da_head, g_cq, w_uq,
                   g_ckv, w_uk, w_uv, w_o, g_ffn, w_up, w_conv, b_conv, w_down):
    row = lambda a: a[l].reshape(1, -1).astype(F32)
    w = w_in[l]
    o1 = D_MODEL
    o2 = o1 + D_MODEL
    o3 = o2 + D_MODEL
    o4 = o3 + Q_LORA
    o5 = o4 + KV_LORA
    o6 = o5 + ROPE_DIM
    uq = w_uq[l]
    pad_rope = lambda a: jnp.pad(a, [(0, 0)] * (a.ndim - 1) + [(0, HEAD_W - ROPE_DIM)])
    return {
        "g_attn": row(g_attn),
        "w_q": w[:, :o1].astype(BF16),
        "w_k": w[:, o1:o2].astype(BF16),
        "w_v": w[:, o2:o3].astype(BF16),
        "w_cq": w[:, o3:o4].astype(BF16),
        "w_ckv": w[:, o4:o5].astype(BF16),
        "w_kr": pad_rope(w[:, o5:o6]).astype(BF16),
        "w_g": w[:, o6:].astype(BF16),
        "b_gate": row(b_gate),
        "g_cq": row(g_cq),
        "g_ckv": row(g_ckv),
        "w_uqn": uq[:, :, :NOPE_DIM].reshape(Q_LORA, HEADS * NOPE_DIM).astype(BF16),
        "w_uqr": pad_rope(uq[:, :, NOPE_DIM:]).reshape(Q_LORA, HEADS * HEAD_W).astype(BF16),
        "w_ukf": jnp.transpose(w_uk[l], (1, 0, 2)).reshape(KV_LORA, HEADS * NOPE_DIM).astype(BF16),
        "w_uvf": jnp.transpose(w_uv[l], (1, 0, 2)).reshape(KV_LORA, HEADS * V_HEAD_DIM).astype(BF16),
        "w_ukt": jnp.transpose(w_uk[l], (0, 2, 1)).astype(BF16),
        "w_uv": w_uv[l].astype(BF16),
        "lam_q1": row(lam_q1), "lam_k1": row(lam_k1), "lam_q2": row(lam_q2), "lam_k2": row(lam_k2),
        "g_da_head": row(g_da_head),
        "g_da_col": g_da_head[l].reshape(-1, 1).astype(F32),
        "w_o": w_o[l].astype(BF16),
        "g_ffn": row(g_ffn),
        "w_up": w_up[l].astype(BF16),
        "w_conv": w_conv[l].astype(F32),
        "b_conv": row(b_conv),
        "w_down": w_down[l].astype(BF16),
    }


def _all_keys_visible(q_pos, k_pos):
    return bool(np.all((k_pos[None, :] // CHUNK) <= (q_pos[:, None] // CHUNK)))


def kernel(x_prompt, x_sample, cache_dk, cache_dv, cache_ckv, cache_krope, state_conv, g_attn, w_in, b_gate, lam_q1, lam_k1, lam_q2, lam_k2, g_da_head, g_cq, w_uq, g_ckv, w_uk, w_uv, w_o, g_ffn, w_up, w_conv, b_conv, w_down, g_final):
    depth = w_in.shape[0]
    B, S, _ = x_prompt.shape
    Bs, Ss, _ = x_sample.shape
    past = cache_dk.shape[2]
    assert _all_keys_visible(past + np.arange(Ss), np.arange(past + Ss))

    gfin = g_final.reshape(1, -1).astype(F32)
    pos_p = jnp.arange(S, dtype=jnp.int32)
    pos_s = past + jnp.arange(Ss, dtype=jnp.int32)
    tm_in = min(S, 256)
    tm_ffn = min(S, 512)
    zero_state = jnp.zeros((B, CONV_W - 1, 2 * D_FF), F32)
    cache_dk2 = cache_dk.reshape(depth, Bs, past * DA_HEADS, DA_V_DIM)
    cache_dv2 = cache_dv.reshape(depth, Bs, past * DA_HEADS, DA_V_DIM)

    yp, ys = x_prompt, x_sample
    state_p = state_s = None
    conv_p, conv_s = [], []
    for l in range(depth):
        lw = _layer_weights(l, g_attn, w_in, b_gate, lam_q1, lam_k1, lam_q2, lam_k2, g_da_head,
                            g_cq, w_uq, g_ckv, w_uk, w_uv, w_o, g_ffn, w_up, w_conv, b_conv, w_down)
        lam_init = 0.8 - 0.6 * math.exp(-0.3 * l)
        final = l == depth - 1

        state_p, (q_da, k_b, v_b, gates, q_m, k_m, v_m) = _in_projection(
            yp, pos_p, lw, l, depth, state_p, absorbed=False, tm=tm_in)
        o_da = _prompt_diff_attention(q_da, k_b, v_b, lw, lam_init)
        o_mla = _prompt_mla_attention(q_m, k_m, v_m)
        yp, cv = _merge_ffn(yp, o_da, o_mla, gates, zero_state, lw, gfin, final=final, tm=tm_ffn)
        conv_p.append(cv)

        state_s, (q_da, k_b, v_b, gates, q_abs) = _in_projection(
            ys, pos_s, lw, l, depth, state_s, absorbed=True, tm=Ss)
        o_da = _sample_diff_attention(q_da, cache_dk2, cache_dv2, k_b, v_b, l, lw, lam_init)
        o_mla = _sample_mla_attention(q_abs, cache_ckv, cache_krope, state_s[2], state_s[3], l, lw)
        ys, cv = _merge_ffn(ys, o_da, o_mla, gates, state_conv[l], lw, gfin, final=final, tm=Ss)
        conv_s.append(cv)

    dk_p, dv_p, ckv_p, kr_p = state_p
    dk_s, dv_s, ckv_s, kr_s = state_s
    heads = lambda a: a.reshape(a.shape[:2] + (-1, DA_HEADS, DA_V_DIM))
    return (yp, ys, heads(dk_p), heads(dv_p), ckv_p, kr_p, jnp.stack(conv_p),
            heads(dk_s), heads(dv_s), ckv_s, kr_s, jnp.stack(conv_s))
```

```python
import functools
import math

import numpy as np
import jax
import jax.numpy as jnp
from jax import lax
from jax.experimental import pallas as pl
from jax.experimental.pallas import tpu as pltpu

F32 = jnp.float32
BF16 = jnp.bfloat16

V7X_VMEM_BYTES = 64 * 1024 * 1024
VMEM_LIMIT_BYTES = V7X_VMEM_BYTES - 8 * 1024 * 1024

D_MODEL = 1024
CHUNK = 64
ROPE_THETA = 10000.0
EPS = 1e-6
DA_HEAD_DIM = 64
DA_V_DIM = 2 * DA_HEAD_DIM
DA_HEADS = D_MODEL // DA_V_DIM
V_HEAD_DIM = 128
MLA_HEADS = D_MODEL // V_HEAD_DIM
NOPE_DIM = 128
ROPE_DIM = 64
Q_LORA = 384
KV_LORA = 256
MLA_SCALE = (NOPE_DIM + ROPE_DIM) ** -0.5
DA_SCALE = DA_HEAD_DIM ** -0.5
D_FF = 2816
CONV_W = 3
HEADS = 8
assert DA_HEADS == HEADS and MLA_HEADS == HEADS
HEAD_W = 128
MLA_QK_W = 2 * HEAD_W
BF16_SUBLANES = 16
V_ROWS = HEAD_W + BF16_SUBLANES
FF_CHUNK = 256
N_FF_CHUNKS = D_FF // FF_CHUNK
assert N_FF_CHUNKS * FF_CHUNK == D_FF

LOG2E = math.log2(math.e)
MASK_VALUE = -1e30


def _cparams(n_grid):
    return pltpu.CompilerParams(
        dimension_semantics=("arbitrary",) * n_grid,
        vmem_limit_bytes=VMEM_LIMIT_BYTES,
    )


def _const_spec(shape):
    nd = len(shape)
    return pl.BlockSpec(shape, lambda *_: (0,) * nd, pipeline_mode=pl.Buffered(1))


def _dot(a, b):
    return jnp.dot(a, b, preferred_element_type=F32)


def _dot_nt(a, b):
    return lax.dot_general(a, b, (((1,), (1,)), ((), ())), preferred_element_type=F32)


def _rms(x, g):
    return x * lax.rsqrt(jnp.mean(x * x, axis=-1, keepdims=True) + EPS) * g


N_STATE = 4


def _inproj_kernel(*refs, absorbed, aliased, tm):
    n_in = 18 if absorbed else 19
    ins = refs[:n_in]
    outs = refs[n_in + (N_STATE if aliased else 0):]
    (x_ref, g_ref, wq_ref, wk_ref, wv_ref, wcq_ref, wckv_ref, wkr_ref, wg_ref, bg_ref,
     gcq_ref, gckv_ref, cos_ref, sa_ref, sb_ref, wuqn_ref, wuqr_ref) = ins[:17]
    kf_ref, vf_ref, ckv_ref, kr_ref = outs[:N_STATE]
    qda_ref, kb_ref, vb_ref, gate_ref, qm_ref = outs[N_STATE:N_STATE + 5]

    x = x_ref[0]
    h = _rms(x, g_ref[...]).astype(BF16)
    cos_t = cos_ref[...]
    sin_a = sa_ref[...]
    sin_b = sb_ref[...]

    def rope(t):
        return (t * cos_t + pltpu.roll(t, HEAD_W - 32, 1) * sin_a
                + pltpu.roll(t, 32, 1) * sin_b)

    def head(t, i):
        return t[:, i * HEAD_W:(i + 1) * HEAD_W]

    def put_v(ref, i, t):
        if absorbed:
            ref[0, i] = t.astype(BF16)
        else:
            ref[0, i, :HEAD_W, :] = t.T.astype(BF16)
            ref[0, i, HEAD_W:, :] = jnp.ones((V_ROWS - HEAD_W, tm), BF16)

    cq = _rms(_dot(h, wcq_ref[...]), gcq_ref[...]).astype(BF16)
    ckv = _rms(_dot(h, wckv_ref[...]), gckv_ref[...])
    ckv_ref[0, 0] = ckv
    ckv_b = ckv.astype(BF16)

    qn = _dot(cq, wuqn_ref[...])
    qr = _dot(cq, wuqr_ref[...])
    if absorbed:
        wukt_ref = ins[17]
        for i in range(HEADS):
            ql = _dot(head(qn, i).astype(BF16), wukt_ref[i])
            qm_ref[0, i, :, :KV_LORA] = (ql * (MLA_SCALE * LOG2E)).astype(BF16)
            qm_ref[0, i, :, KV_LORA:] = (rope(head(qr, i)) * (MLA_SCALE * LOG2E)).astype(BF16)
    else:
        wukf_ref, wuvf_ref = ins[17:19]
        km_ref, vm_ref = outs[N_STATE + 5:N_STATE + 7]
        kn = _dot(ckv_b, wukf_ref[...])
        vm = _dot(ckv_b, wuvf_ref[...])
        for i in range(HEADS):
            qm_ref[0, i, :, :HEAD_W] = (head(qn, i) * (MLA_SCALE * LOG2E)).astype(BF16)
            qm_ref[0, i, :, HEAD_W:] = (rope(head(qr, i)) * (MLA_SCALE * LOG2E)).astype(BF16)
            km_ref[0, i, :, :HEAD_W] = head(kn, i).astype(BF16)
            put_v(vm_ref, i, head(vm, i))

    zq = _dot(h, wq_ref[...])
    for i in range(HEADS):
        qda_ref[0, i] = (rope(head(zq, i)) * (DA_SCALE * LOG2E)).astype(BF16)

    zk = _dot(h, wk_ref[...])
    for i in range(HEADS):
        r = rope(head(zk, i))
        kf_ref[0, 0, pl.ds(i, tm, stride=HEADS), :] = r
        kb_ref[0, i] = r.astype(BF16)

    zv = _dot(h, wv_ref[...])
    for i in range(HEADS):
        vf_ref[0, 0, pl.ds(i, tm, stride=HEADS), :] = head(zv, i)
        put_v(vb_ref, i, head(zv, i))

    gate_ref[0] = jax.nn.sigmoid(_dot(h, wg_ref[...]) + bg_ref[...]).astype(BF16)

    kr = rope(_dot(h, wkr_ref[...]))
    kr_ref[0, 0] = kr[:, :ROPE_DIM]
    if not absorbed:
        kr_b = kr.astype(BF16)
        for i in range(HEADS):
            km_ref[0, i, :, HEAD_W:] = kr_b


def _rope_tables(pos):
    half = ROPE_DIM // 2
    inv = jnp.power(ROPE_THETA, -jnp.arange(half, dtype=F32) / half)
    ang = pos.astype(F32)[:, None] * inv[None, :]
    cos, sin = jnp.cos(ang), jnp.sin(ang)
    zero = jnp.zeros_like(sin)
    reps = HEAD_W // ROPE_DIM
    cos_t = jnp.tile(jnp.concatenate([cos, cos], axis=1), (1, reps))
    sin_a = jnp.tile(jnp.concatenate([-sin, zero], axis=1), (1, reps))
    sin_b = jnp.tile(jnp.concatenate([zero, sin], axis=1), (1, reps))
    return cos_t, sin_a, sin_b


def _in_projection(x, pos, lw, layer, depth, state, *, absorbed, tm):
    B, S, _ = x.shape
    assert S % tm == 0
    ns = S // tm
    cos_t, sin_a, sin_b = _rope_tables(pos)

    tok = lambda w: pl.BlockSpec((1, tm, w), lambda b, s: (b, s, 0))
    hd = lambda w: pl.BlockSpec((1, HEADS, tm, w), lambda b, s: (b, 0, s, 0))
    hdt = pl.BlockSpec((1, HEADS, V_ROWS, tm), lambda b, s: (b, 0, 0, s))
    vspec = hd(HEAD_W) if absorbed else hdt
    vshape = (B, HEADS, S, HEAD_W) if absorbed else (B, HEADS, V_ROWS, S)
    st = lambda rows, w: pl.BlockSpec((1, 1, rows, w), lambda b, s: (layer, b, s, 0))
    tab = pl.BlockSpec((tm, HEAD_W), lambda b, s: (s, 0))

    consts = [lw["g_attn"], lw["w_q"], lw["w_k"], lw["w_v"], lw["w_cq"], lw["w_ckv"], lw["w_kr"],
              lw["w_g"], lw["b_gate"], lw["g_cq"], lw["g_ckv"]]
    mla_w = [lw["w_uqn"], lw["w_uqr"]] + ([lw["w_ukt"]] if absorbed else [lw["w_ukf"], lw["w_uvf"]])
    ins = [x] + consts + [cos_t, sin_a, sin_b] + mla_w
    in_specs = ([tok(D_MODEL)] + [_const_spec(a.shape) for a in consts] + [tab, tab, tab]
                + [_const_spec(a.shape) for a in mla_w])
    aliases = {}
    if state is not None:
        for i, a in enumerate(state):
            aliases[len(ins)] = i
            ins.append(a)
            in_specs.append(pl.BlockSpec(memory_space=pl.ANY))

    sds = jax.ShapeDtypeStruct
    qm_w = KV_LORA + HEAD_W if absorbed else MLA_QK_W
    state_dims = ((S * HEADS, tm * HEADS, HEAD_W), (S * HEADS, tm * HEADS, HEAD_W),
                  (S, tm, KV_LORA), (S, tm, ROPE_DIM))
    out_shape = [sds((depth, B, rows, w), F32) for rows, _, w in state_dims] + [
        sds((B, HEADS, S, HEAD_W), BF16),
        sds((B, HEADS, S, HEAD_W), BF16),
        sds(vshape, BF16),
        sds((B, S, 2 * D_MODEL), BF16),
        sds((B, HEADS, S, qm_w), BF16),
    ]
    out_specs = [st(blk, w) for _, blk, w in state_dims] + [hd(HEAD_W), hd(HEAD_W), vspec,
                                                            tok(2 * D_MODEL), hd(qm_w)]
    if not absorbed:
        out_shape += [sds((B, HEADS, S, MLA_QK_W), BF16), sds(vshape, BF16)]
        out_specs += [hd(MLA_QK_W), vspec]

    outs = pl.pallas_call(
        functools.partial(_inproj_kernel, absorbed=absorbed, aliased=state is not None, tm=tm),
        grid=(B, ns),
        in_specs=in_specs,
        out_specs=out_specs,
        out_shape=out_shape,
        input_output_aliases=aliases,
        compiler_params=_cparams(2),
        name="in_projection_absorbed" if absorbed else "in_projection",
    )(*ins)
    return tuple(outs[:N_STATE]), tuple(outs[N_STATE:])


def _lambda(lq1_ref, lk1_ref, lq2_ref, lk2_ref, lam_init):
    a = jnp.sum(lq1_ref[...] * lk1_ref[...], axis=-1, keepdims=True)
    b = jnp.sum(lq2_ref[...] * lk2_ref[...], axis=-1, keepdims=True)
    return jnp.exp(a) - jnp.exp(b) + lam_init


def _diff_finish(o1, o2, lam, g, lam_init):
    o = o1 - lam * o2
    return _rms(o, g) * (1.0 - lam_init)


def _key_blocks(qi, tq, tk):
    n_full = (tq * qi) // tk
    blocks = [(tk * jj, tk) for jj in range(n_full)]
    blocks.append((tk * n_full, tq * (qi + 1) - tk * n_full))
    return blocks


def _prompt_attn_kernel(*refs, tq, tk, lookahead, n_q, n_heads, diff, lam_init):
    if diff:
        lq1_ref, lk1_ref, lq2_ref, lk2_ref, g_ref, q_ref, k_ref, vt_ref, o_ref = refs
    else:
        q_ref, k_ref, vt_ref, o_ref = refs
    n_br = 2 if diff else 1
    queries = {}
    masks = {}

    def branch_queries(hd, qi):
        if (hd, qi) not in queries:
            q = q_ref[0, hd, qi * tq:(qi + 1) * tq, :]
            if diff:
                lane = lax.broadcasted_iota(jnp.int32, q.shape, 1)
                zero = jnp.zeros_like(q)
                queries[hd, qi] = (jnp.where(lane < DA_HEAD_DIM, q, zero),
                                   jnp.where(lane >= DA_HEAD_DIM, q, zero))
            else:
                queries[hd, qi] = (q,)
        return queries[hd, qi]

    def diag_mask():
        if not masks:
            kc = lax.broadcasted_iota(jnp.int32, (tq, tq), 0) // CHUNK
            qc = lax.broadcasted_iota(jnp.int32, (tq, tq), 1) // CHUNK
            masks[tq] = kc <= qc
        return masks[tq]

    def scores(task):
        hd, qi, start, size, last, b = task
        s = _dot_nt(k_ref[0, hd, start:start + size, :], branch_queries(hd, qi)[b])
        if not last:
            return s
        tail = jnp.where(diag_mask(), s[size - tq:], MASK_VALUE)
        return tail if size == tq else jnp.concatenate([s[:size - tq], tail], axis=0)

    tasks = []
    for hd in range(n_heads):
        for qi in range(n_q):
            blocks = _key_blocks(qi, tq, tk)
            for start, size in blocks:
                for b in range(n_br):
                    tasks.append((hd, qi, start, size, (start, size) == blocks[-1], b))
    m = [None] * n_br
    acc = [None] * n_br
    pending = [scores(task) for task in tasks[:lookahead]]
    for n, (hd, qi, start, size, last, b) in enumerate(tasks):
        s = pending.pop(0)
        if n + lookahead < len(tasks):
            pending.append(scores(tasks[n + lookahead]))
        vt = vt_ref[0, hd, :, start:start + size]
        s_max = jnp.max(s, axis=0, keepdims=True)
        if start == 0:
            m[b] = s_max
            acc[b] = _dot(vt, jnp.exp2(s - s_max).astype(BF16))
        else:
            m_new = jnp.maximum(m[b], s_max)
            acc[b] = jnp.exp2(m[b] - m_new) * acc[b] + _dot(vt, jnp.exp2(s - m_new).astype(BF16))
            m[b] = m_new
        if last and b == n_br - 1:
            outs = [a[:HEAD_W] / a[HEAD_W:HEAD_W + 1] for a in acc]
            if diff:
                lam = _lambda(lq1_ref, lk1_ref, lq2_ref, lk2_ref, lam_init)
                o_t = outs[0] - lam * outs[1]
                o_t = (o_t * lax.rsqrt(jnp.mean(o_t * o_t, axis=0, keepdims=True) + EPS)
                       * g_ref[...] * (1.0 - lam_init))
            else:
                o_t = outs[0]
            o_ref[0, qi * tq:(qi + 1) * tq, hd * HEAD_W:(hd + 1) * HEAD_W] = o_t.T.astype(BF16)


ATTN_TQ = 256
ATTN_TK = 512
ATTN_LOOKAHEAD = 6
ATTN_HEADS_PER_STEP = 4


def _prompt_attention(q, k, vt, small, *, diff, lam_init, name):
    B, H, S, dq = q.shape
    tq, tk = min(S, ATTN_TQ), min(S, ATTN_TK)
    hps = ATTN_HEADS_PER_STEP
    assert S % tq == 0 and S % tk == 0 and tk % tq == 0 and tq % CHUNK == 0 and H % hps == 0
    qk = pl.BlockSpec((1, hps, S, dq), lambda b, h: (b, h, 0, 0))
    return pl.pallas_call(
        functools.partial(_prompt_attn_kernel, tq=tq, tk=tk, lookahead=ATTN_LOOKAHEAD,
                          n_q=S // tq, n_heads=hps, diff=diff, lam_init=lam_init),
        grid=(B, H // hps),
        in_specs=[_const_spec(a.shape) for a in small]
        + [qk, qk, pl.BlockSpec((1, hps, V_ROWS, S), lambda b, h: (b, h, 0, 0))],
        out_specs=pl.BlockSpec((1, S, hps * HEAD_W), lambda b, h: (b, 0, h)),
        out_shape=jax.ShapeDtypeStruct((B, S, D_MODEL), BF16),
        compiler_params=_cparams(2),
        name=name,
    )(*small, q, k, vt)


def _prompt_diff_attention(q, k, vt, lw, lam_init):
    small = [lw["lam_q1"], lw["lam_k1"], lw["lam_q2"], lw["lam_k2"], lw["g_da_col"]]
    return _prompt_attention(q, k, vt, small, diff=True, lam_init=lam_init, name="diff_attention")


def _prompt_mla_attention(q, k, vt):
    return _prompt_attention(q, k, vt, [], diff=False, lam_init=None, name="mla_attention")


def _two_part_softmax(s_c, s_n):
    m = jnp.maximum(jnp.max(s_c, axis=-1, keepdims=True), jnp.max(s_n, axis=-1, keepdims=True))
    p_c = jnp.exp2(s_c - m)
    p_n = jnp.exp2(s_n - m)
    l = jnp.sum(p_c, axis=-1, keepdims=True) + jnp.sum(p_n, axis=-1, keepdims=True)
    return p_c.astype(BF16), p_n.astype(BF16), l


def _diff_decode_kernel(lq1_ref, lk1_ref, lq2_ref, lk2_ref, g_ref, q_ref, kc_ref, vc_ref,
                        kn_ref, vn_ref, o_ref, *, lam_init, t_cache):
    lam = _lambda(lq1_ref, lk1_ref, lq2_ref, lk2_ref, lam_init)
    for h in range(HEADS):
        q = q_ref[0, h]
        lane = lax.broadcasted_iota(jnp.int32, q.shape, 1)
        zero = jnp.zeros_like(q)
        kc = kc_ref[0, 0, pl.ds(h, t_cache, stride=HEADS), :].astype(BF16)
        vc = vc_ref[0, 0, pl.ds(h, t_cache, stride=HEADS), :].astype(BF16)
        kn = kn_ref[0, h]
        vn = vn_ref[0, h]
        outs = []
        for qq in (jnp.where(lane < DA_HEAD_DIM, q, zero), jnp.where(lane >= DA_HEAD_DIM, q, zero)):
            p_c, p_n, l = _two_part_softmax(_dot_nt(qq, kc), _dot_nt(qq, kn))
            outs.append((_dot(p_c, vc) + _dot(p_n, vn)) / l)
        o_ref[0, :, h * HEAD_W:(h + 1) * HEAD_W] = _diff_finish(
            outs[0], outs[1], lam, g_ref[...], lam_init).astype(BF16)


def _sample_diff_attention(q, cache_k, cache_v, k_new, v_new, layer, lw, lam_init):
    B = cache_k.shape[1]
    H = q.shape[1]
    S = q.shape[2] // B
    rows = cache_k.shape[2]
    small = [lw["lam_q1"], lw["lam_k1"], lw["lam_q2"], lw["lam_k2"], lw["g_da_head"]]
    new = pl.BlockSpec((1, H, S, HEAD_W), lambda b: (0, 0, b, 0))
    cache = pl.BlockSpec((1, 1, rows, HEAD_W), lambda b: (layer, b, 0, 0))
    return pl.pallas_call(
        functools.partial(_diff_decode_kernel, lam_init=lam_init, t_cache=rows // H),
        grid=(B,),
        in_specs=[_const_spec(a.shape) for a in small] + [new, cache, cache, new, new],
        out_specs=pl.BlockSpec((1, S, D_MODEL), lambda b: (b, 0, 0)),
        out_shape=jax.ShapeDtypeStruct((B, S, D_MODEL), BF16),
        compiler_params=_cparams(1),
        name="diff_attention_decode",
    )(*small, q, cache_k, cache_v, k_new, v_new)


def _mla_decode_kernel(q_ref, cc_ref, rc_ref, cn_ref, rn_ref, wuv_ref, o_ref, *, s_len):
    q = q_ref[0].reshape(HEADS * s_len, q_ref.shape[-1])
    ql = q[:, :KV_LORA]
    qr = q[:, KV_LORA:KV_LORA + ROPE_DIM]
    cc = cc_ref[0, 0].astype(BF16)
    cn = cn_ref[0, 0].astype(BF16)
    s_c = _dot_nt(ql, cc) + _dot_nt(qr, rc_ref[0, 0].astype(BF16))
    s_n = _dot_nt(ql, cn) + _dot_nt(qr, rn_ref[0, 0].astype(BF16))
    p_c, p_n, l = _two_part_softmax(s_c, s_n)
    o_lat = ((_dot(p_c, cc) + _dot(p_n, cn)) / l).astype(BF16)
    for i in range(HEADS):
        o_ref[0, :, i * HEAD_W:(i + 1) * HEAD_W] = _dot(
            o_lat[i * s_len:(i + 1) * s_len], wuv_ref[i]).astype(BF16)


def _sample_mla_attention(q_abs, cache_ckv, cache_kr, new_ckv, new_kr, layer, lw):
    B = cache_ckv.shape[1]
    _, H, rows, W = q_abs.shape
    S = rows // B
    cached = lambda a: pl.BlockSpec((1, 1) + a.shape[2:], lambda b: (layer, b, 0, 0))
    new = lambda a: pl.BlockSpec((1, 1, S, a.shape[3]), lambda b: (layer, 0, b, 0))
    return pl.pallas_call(
        functools.partial(_mla_decode_kernel, s_len=S),
        grid=(B,),
        in_specs=[pl.BlockSpec((1, H, S, W), lambda b: (0, 0, b, 0)),
                  cached(cache_ckv), cached(cache_kr), new(new_ckv), new(new_kr),
                  _const_spec(lw["w_uv"].shape)],
        out_specs=pl.BlockSpec((1, S, D_MODEL), lambda b: (b, 0, 0)),
        out_shape=jax.ShapeDtypeStruct((B, S, D_MODEL), BF16),
        compiler_params=_cparams(1),
        name="mla_attention_decode",
    )(q_abs, cache_ckv, cache_kr, new_ckv, new_kr, lw["w_uv"])


F32_SUBLANES = 8


def _merge_ffn_kernel(x_ref, oda_ref, omla_ref, gate_ref, prev_ref, wo_ref, gffn_ref, wup_ref,
                      wconv_ref, bconv_ref, wdown_ref, gfin_ref, y_ref, conv_ref,
                      carry, xbuf, act, *, tm, final):
    sub = F32_SUBLANES
    nv = tm // sub
    n_slab = D_MODEL // HEAD_W

    @pl.when(pl.program_id(1) == 0)
    def _():
        carry[...] = prev_ref[0]

    def to_slabs(val):
        for k in range(n_slab):
            xbuf[k] = val[:, k * HEAD_W:(k + 1) * HEAD_W]

    def from_slabs(row_tile):
        return jnp.concatenate(
            [jnp.concatenate([row_tile(k, c) for k in range(n_slab)], axis=1) for c in range(nv)],
            axis=0)

    gates = gate_ref[0]
    merged = gates[:, :D_MODEL] * oda_ref[0] + gates[:, D_MODEL:] * omla_ref[0]
    to_slabs(x_ref[0] + _dot(merged, wo_ref[...]))
    x1 = from_slabs(lambda k, c: xbuf[k, pl.ds(c, sub, stride=nv), :])
    h = _rms(x1, gffn_ref[...]).astype(BF16)
    first_row = lax.broadcasted_iota(jnp.int32, (sub, FF_CHUNK), 0) == 0

    def conv_cols(c0):
        cols = slice(c0, c0 + FF_CHUNK)
        u = _dot(h, wup_ref[:, cols])
        e1 = jnp.where(first_row, carry[1:2, cols], pltpu.roll(u[tm - sub:], 1, 0))
        e2 = jnp.where(first_row, carry[0:1, cols], pltpu.roll(u[tm - 2 * sub:tm - sub], 1, 0))
        prev1 = jnp.concatenate([e1, u[:tm - sub]], axis=0)
        prev2 = jnp.concatenate([e2, e1, u[:tm - 2 * sub]], axis=0)
        c = (bconv_ref[:, cols] + wconv_ref[0:1, cols] * prev2 + wconv_ref[1:2, cols] * prev1
             + wconv_ref[2:3, cols] * u)
        tail = jnp.concatenate([u[tm - sub - 1:tm - sub], u[tm - 1:tm]], axis=0)
        carry[:, cols] = tail
        conv_ref[0, :, cols] = tail
        return c

    for j in range(N_FF_CHUNKS):
        ca = conv_cols(j * FF_CHUNK)
        cb = conv_cols(D_FF + j * FF_CHUNK)
        act[:, j * FF_CHUNK:(j + 1) * FF_CHUNK] = (jax.nn.silu(ca) * cb).astype(BF16)

    x2 = x1 + _dot(act[...], wdown_ref[...])
    if final:
        x2 = _rms(x2, gfin_ref[...])
    to_slabs(x2)

    def natural_tile(k, tile):
        t0 = tile * sub
        if nv % sub == 0:
            return xbuf[k, pl.ds((t0 % nv) * sub + t0 // nv, sub, stride=sub), :]
        return jnp.concatenate(
            [xbuf[k, pl.ds(t0 // nv + g, nv, stride=sub), :] for g in range(sub // nv)], axis=0)

    y_ref[0] = from_slabs(natural_tile)


def _merge_ffn(x, o_da, o_mla, gates, prev_conv, lw, g_final, *, final, tm):
    B, S, _ = x.shape
    nv = tm // F32_SUBLANES
    assert S % tm == 0 and tm % BF16_SUBLANES == 0 and nv >= CONV_W - 1
    assert nv % F32_SUBLANES == 0 or F32_SUBLANES % nv == 0
    tok = lambda w: pl.BlockSpec((1, tm, w), lambda b, s: (b, s, 0))
    state = pl.BlockSpec((1, CONV_W - 1, 2 * D_FF), lambda b, s: (b, 0, 0))
    consts = [lw["w_o"], lw["g_ffn"], lw["w_up"], lw["w_conv"], lw["b_conv"], lw["w_down"], g_final]
    return pl.pallas_call(
        functools.partial(_merge_ffn_kernel, tm=tm, final=final),
        grid=(B, S // tm),
        in_specs=[tok(D_MODEL), tok(D_MODEL), tok(D_MODEL), tok(2 * D_MODEL), state]
        + [_const_spec(a.shape) for a in consts],
        out_specs=[tok(D_MODEL), state],
        out_shape=[jax.ShapeDtypeStruct((B, S, D_MODEL), F32),
                   jax.ShapeDtypeStruct((B, CONV_W - 1, 2 * D_FF), F32)],
        scratch_shapes=[pltpu.VMEM((CONV_W - 1, 2 * D_FF), F32),
                        pltpu.VMEM((D_MODEL // HEAD_W, tm, HEAD_W), F32),
                        pltpu.VMEM((tm, D_FF), BF16)],
        compiler_params=_cparams(2),
        name="merge_ffn_final" if final else "merge_ffn",
    )(x, o_da, o_mla, gates, prev_conv, *consts)


def _layer_weights(l, g_attn, w_in, b_gate, lam_q1, lam_k1, lam_q2, lam_k2, g_da_head, g_cq, w_uq,
                   g_ckv, w_uk, w_uv, w_o, g_ffn, w_up, w_conv, b_conv, w_down):
    row = lambda a: a[l].reshape(1, -1).astype(F32)
    w = w_in[l]
    o1 = D_MODEL
    o2 = o1 + D_MODEL
    o3 = o2 + D_MODEL
    o4 = o3 + Q_LORA
    o5 = o4 + KV_LORA
    o6 = o5 + ROPE_DIM
    uq = w_uq[l]
    pad_rope = lambda a: jnp.pad(a, [(0, 0)] * (a.ndim - 1) + [(0, HEAD_W - ROPE_DIM)])
    return {
        "g_attn": row(g_attn),
        "w_q": w[:, :o1].astype(BF16),
        "w_k": w[:, o1:o2].astype(BF16),
        "w_v": w[:, o2:o3].astype(BF16),
        "w_cq": w[:, o3:o4].astype(BF16),
        "w_ckv": w[:, o4:o5].astype(BF16),
        "w_kr": pad_rope(w[:, o5:o6]).astype(BF16),
        "w_g": w[:, o6:].astype(BF16),
        "b_gate": row(b_gate),
        "g_cq": row(g_cq),
        "g_ckv": row(g_ckv),
        "w_uqn": uq[:, :, :NOPE_DIM].reshape(Q_LORA, HEADS * NOPE_DIM).astype(BF16),
        "w_uqr": pad_rope(uq[:, :, NOPE_DIM:]).reshape(Q_LORA, HEADS * HEAD_W).astype(BF16),
        "w_ukf": jnp.transpose(w_uk[l], (1, 0, 2)).reshape(KV_LORA, HEADS * NOPE_DIM).astype(BF16),
        "w_uvf": jnp.transpose(w_uv[l], (1, 0, 2)).reshape(KV_LORA, HEADS * V_HEAD_DIM).astype(BF16),
        "w_ukt": jnp.transpose(w_uk[l], (0, 2, 1)).astype(BF16),
        "w_uv": w_uv[l].astype(BF16),
        "lam_q1": row(lam_q1), "lam_k1": row(lam_k1), "lam_q2": row(lam_q2), "lam_k2": row(lam_k2),
        "g_da_head": row(g_da_head),
        "g_da_col": g_da_head[l].reshape(-1, 1).astype(F32),
        "w_o": w_o[l].astype(BF16),
        "g_ffn": row(g_ffn),
        "w_up": w_up[l].astype(BF16),
        "w_conv": w_conv[l].astype(F32),
        "b_conv": row(b_conv),
        "w_down": w_down[l].astype(BF16),
    }


def _all_keys_visible(q_pos, k_pos):
    return bool(np.all((k_pos[None, :] // CHUNK) <= (q_pos[:, None] // CHUNK)))


def kernel(x_prompt, x_sample, cache_dk, cache_dv, cache_ckv, cache_krope, state_conv, g_attn, w_in, b_gate, lam_q1, lam_k1, lam_q2, lam_k2, g_da_head, g_cq, w_uq, g_ckv, w_uk, w_uv, w_o, g_ffn, w_up, w_conv, b_conv, w_down, g_final):
    depth = w_in.shape[0]
    B, S, _ = x_prompt.shape
    Bs, Ss, _ = x_sample.shape
    past = cache_dk.shape[2]
    assert _all_keys_visible(past + np.arange(Ss), np.arange(past + Ss))

    gfin = g_final.reshape(1, -1).astype(F32)
    pos_p = jnp.arange(S, dtype=jnp.int32)
    pos_s = past + jnp.arange(Ss, dtype=jnp.int32)
    tm_in = min(S, 256)
    tm_ffn = min(S, 512)
    zero_state = jnp.zeros((B, CONV_W - 1, 2 * D_FF), F32)
    cache_dk2 = cache_dk.reshape(depth, Bs, past * DA_HEADS, DA_V_DIM)
    cache_dv2 = cache_dv.reshape(depth, Bs, past * DA_HEADS, DA_V_DIM)

    yp, ys = x_prompt, x_sample
    state_p = state_s = None
    conv_p, conv_s = [], []
    for l in range(depth):
        lw = _layer_weights(l, g_attn, w_in, b_gate, lam_q1, lam_k1, lam_q2, lam_k2, g_da_head,
                            g_cq, w_uq, g_ckv, w_uk, w_uv, w_o, g_ffn, w_up, w_conv, b_conv, w_down)
        lam_init = 0.8 - 0.6 * math.exp(-0.3 * l)
        final = l == depth - 1

        state_p, (q_da, k_b, v_b, gates, q_m, k_m, v_m) = _in_projection(
            yp, pos_p, lw, l, depth, state_p, absorbed=False, tm=tm_in)
        o_da = _prompt_diff_attention(q_da, k_b, v_b, lw, lam_init)
        o_mla = _prompt_mla_attention(q_m, k_m, v_m)
        yp, cv = _merge_ffn(yp, o_da, o_mla, gates, zero_state, lw, gfin, final=final, tm=tm_ffn)
        conv_p.append(cv)

        state_s, (q_da, k_b, v_b, gates, q_abs) = _in_projection(
            ys.reshape(1, Bs * Ss, D_MODEL), jnp.tile(pos_s, Bs), lw, l, depth, state_s,
            absorbed=True, tm=Bs * Ss)
        o_da = _sample_diff_attention(q_da, cache_dk2, cache_dv2, k_b, v_b, l, lw, lam_init)
        o_mla = _sample_mla_attention(q_abs, cache_ckv, cache_krope, state_s[2], state_s[3], l, lw)
        ys, cv = _merge_ffn(ys, o_da, o_mla, gates.reshape(Bs, Ss, 2 * D_MODEL), state_conv[l], lw,
                            gfin, final=final, tm=Ss)
        conv_s.append(cv)

    dk_p, dv_p, ckv_p, kr_p = state_p
    dk_s, dv_s, ckv_s, kr_s = state_s
    heads = lambda a, b, s: a.reshape(depth, b, s, DA_HEADS, DA_V_DIM)
    seqs = lambda a: a.reshape(depth, Bs, Ss, a.shape[-1])
    return (yp, ys, heads(dk_p, B, S), heads(dv_p, B, S), ckv_p, kr_p, jnp.stack(conv_p),
            heads(dk_s, Bs, Ss), heads(dv_s, Bs, Ss), seqs(ckv_s), seqs(kr_s), jnp.stack(conv_s))
```

```python
import functools
import math

import numpy as np
import jax
import jax.numpy as jnp
from jax import lax
from jax.experimental import pallas as pl
from jax.experimental.pallas import tpu as pltpu

F32 = jnp.float32
BF16 = jnp.bfloat16

V7X_VMEM_BYTES = 64 * 1024 * 1024
VMEM_LIMIT_BYTES = V7X_VMEM_BYTES - 8 * 1024 * 1024

D_MODEL = 1024
CHUNK = 64
ROPE_THETA = 10000.0
EPS = 1e-6
DA_HEAD_DIM = 64
DA_V_DIM = 2 * DA_HEAD_DIM
DA_HEADS = D_MODEL // DA_V_DIM
V_HEAD_DIM = 128
MLA_HEADS = D_MODEL // V_HEAD_DIM
NOPE_DIM = 128
ROPE_DIM = 64
Q_LORA = 384
KV_LORA = 256
MLA_SCALE = (NOPE_DIM + ROPE_DIM) ** -0.5
DA_SCALE = DA_HEAD_DIM ** -0.5
D_FF = 2816
CONV_W = 3
HEADS = 8
assert DA_HEADS == HEADS and MLA_HEADS == HEADS
HEAD_W = 128
MLA_QK_W = 2 * HEAD_W
BF16_SUBLANES = 16
V_ROWS = HEAD_W + BF16_SUBLANES
FF_CHUNK = 256
N_FF_CHUNKS = D_FF // FF_CHUNK
assert N_FF_CHUNKS * FF_CHUNK == D_FF

LOG2E = math.log2(math.e)
MASK_VALUE = -1e30


def _cparams(n_grid):
    return pltpu.CompilerParams(
        dimension_semantics=("arbitrary",) * n_grid,
        vmem_limit_bytes=VMEM_LIMIT_BYTES,
    )


def _const_spec(shape):
    nd = len(shape)
    return pl.BlockSpec(shape, lambda *_: (0,) * nd, pipeline_mode=pl.Buffered(1))


def _dot(a, b):
    return jnp.dot(a, b, preferred_element_type=F32)


def _dot_nt(a, b):
    return lax.dot_general(a, b, (((1,), (1,)), ((), ())), preferred_element_type=F32)


def _rms(x, g):
    return x * lax.rsqrt(jnp.mean(x * x, axis=-1, keepdims=True) + EPS) * g


N_STATE = 4


def _inproj_kernel(*refs, absorbed, aliased, tm):
    n_in = 18 if absorbed else 19
    ins = refs[:n_in]
    outs = refs[n_in + (N_STATE if aliased else 0):]
    (x_ref, g_ref, wq_ref, wk_ref, wv_ref, wcq_ref, wckv_ref, wkr_ref, wg_ref, bg_ref,
     gcq_ref, gckv_ref, cos_ref, sa_ref, sb_ref, wuqn_ref, wuqr_ref) = ins[:17]
    kf_ref, vf_ref, ckv_ref, kr_ref = outs[:N_STATE]
    qda_ref, kb_ref, vb_ref, gate_ref, qm_ref = outs[N_STATE:N_STATE + 5]

    x = x_ref[0]
    h = _rms(x, g_ref[...]).astype(BF16)
    cos_t = cos_ref[...]
    sin_a = sa_ref[...]
    sin_b = sb_ref[...]

    def rope(t):
        return (t * cos_t + pltpu.roll(t, HEAD_W - 32, 1) * sin_a
                + pltpu.roll(t, 32, 1) * sin_b)

    def head(t, i):
        return t[:, i * HEAD_W:(i + 1) * HEAD_W]

    def put_v(ref, i, t):
        if absorbed:
            ref[0, i] = t.astype(BF16)
        else:
            ref[0, i, :HEAD_W, :] = t.T.astype(BF16)
            ref[0, i, HEAD_W:, :] = jnp.ones((V_ROWS - HEAD_W, tm), BF16)

    cq = _rms(_dot(h, wcq_ref[...]), gcq_ref[...]).astype(BF16)
    ckv = _rms(_dot(h, wckv_ref[...]), gckv_ref[...])
    ckv_ref[0, 0] = ckv
    ckv_b = ckv.astype(BF16)

    qn = _dot(cq, wuqn_ref[...])
    qr = _dot(cq, wuqr_ref[...])
    if absorbed:
        wukt_ref = ins[17]
        for i in range(HEADS):
            ql = _dot(head(qn, i).astype(BF16), wukt_ref[i])
            qm_ref[0, i, :, :KV_LORA] = (ql * (MLA_SCALE * LOG2E)).astype(BF16)
            qm_ref[0, i, :, KV_LORA:] = (rope(head(qr, i)) * (MLA_SCALE * LOG2E)).astype(BF16)
    else:
        wukf_ref, wuvf_ref = ins[17:19]
        km_ref, vm_ref = outs[N_STATE + 5:N_STATE + 7]
        kn = _dot(ckv_b, wukf_ref[...])
        vm = _dot(ckv_b, wuvf_ref[...])
        for i in range(HEADS):
            qm_ref[0, i, :, :HEAD_W] = (head(qn, i) * (MLA_SCALE * LOG2E)).astype(BF16)
            qm_ref[0, i, :, HEAD_W:] = (rope(head(qr, i)) * (MLA_SCALE * LOG2E)).astype(BF16)
            km_ref[0, i, :, :HEAD_W] = head(kn, i).astype(BF16)
            put_v(vm_ref, i, head(vm, i))

    zq = _dot(h, wq_ref[...])
    for i in range(HEADS):
        qda_ref[0, i] = (rope(head(zq, i)) * (DA_SCALE * LOG2E)).astype(BF16)

    zk = _dot(h, wk_ref[...])
    for i in range(HEADS):
        r = rope(head(zk, i))
        kf_ref[0, 0, pl.ds(i, tm, stride=HEADS), :] = r
        kb_ref[0, i] = r.astype(BF16)

    zv = _dot(h, wv_ref[...])
    for i in range(HEADS):
        vf_ref[0, 0, pl.ds(i, tm, stride=HEADS), :] = head(zv, i)
        put_v(vb_ref, i, head(zv, i))

    gate_ref[0] = jax.nn.sigmoid(_dot(h, wg_ref[...]) + bg_ref[...]).astype(BF16)

    kr = rope(_dot(h, wkr_ref[...]))
    kr_ref[0, 0] = kr[:, :ROPE_DIM]
    if not absorbed:
        kr_b = kr.astype(BF16)
        for i in range(HEADS):
            km_ref[0, i, :, HEAD_W:] = kr_b


def _rope_tables(pos):
    half = ROPE_DIM // 2
    inv = jnp.power(ROPE_THETA, -jnp.arange(half, dtype=F32) / half)
    ang = pos.astype(F32)[:, None] * inv[None, :]
    cos, sin = jnp.cos(ang), jnp.sin(ang)
    zero = jnp.zeros_like(sin)
    reps = HEAD_W // ROPE_DIM
    cos_t = jnp.tile(jnp.concatenate([cos, cos], axis=1), (1, reps))
    sin_a = jnp.tile(jnp.concatenate([-sin, zero], axis=1), (1, reps))
    sin_b = jnp.tile(jnp.concatenate([zero, sin], axis=1), (1, reps))
    return cos_t, sin_a, sin_b


def _in_projection(x, pos, lw, layer, depth, state, *, absorbed, tm):
    B, S, _ = x.shape
    assert S % tm == 0
    ns = S // tm
    cos_t, sin_a, sin_b = _rope_tables(pos)

    tok = lambda w: pl.BlockSpec((1, tm, w), lambda b, s: (b, s, 0))
    hd = lambda w: pl.BlockSpec((1, HEADS, tm, w), lambda b, s: (b, 0, s, 0))
    hdt = pl.BlockSpec((1, HEADS, V_ROWS, tm), lambda b, s: (b, 0, 0, s))
    vspec = hd(HEAD_W) if absorbed else hdt
    vshape = (B, HEADS, S, HEAD_W) if absorbed else (B, HEADS, V_ROWS, S)
    st = lambda rows, w: pl.BlockSpec((1, 1, rows, w), lambda b, s: (layer, b, s, 0))
    tab = pl.BlockSpec((tm, HEAD_W), lambda b, s: (s, 0))

    consts = [lw["g_attn"], lw["w_q"], lw["w_k"], lw["w_v"], lw["w_cq"], lw["w_ckv"], lw["w_kr"],
              lw["w_g"], lw["b_gate"], lw["g_cq"], lw["g_ckv"]]
    mla_w = [lw["w_uqn"], lw["w_uqr"]] + ([lw["w_ukt"]] if absorbed else [lw["w_ukf"], lw["w_uvf"]])
    ins = [x] + consts + [cos_t, sin_a, sin_b] + mla_w
    in_specs = ([tok(D_MODEL)] + [_const_spec(a.shape) for a in consts] + [tab, tab, tab]
                + [_const_spec(a.shape) for a in mla_w])
    aliases = {}
    if state is not None:
        for i, a in enumerate(state):
            aliases[len(ins)] = i
            ins.append(a)
            in_specs.append(pl.BlockSpec(memory_space=pl.ANY))

    sds = jax.ShapeDtypeStruct
    qm_w = KV_LORA + HEAD_W if absorbed else MLA_QK_W
    state_dims = ((S * HEADS, tm * HEADS, HEAD_W), (S * HEADS, tm * HEADS, HEAD_W),
                  (S, tm, KV_LORA), (S, tm, ROPE_DIM))
    out_shape = [sds((depth, B, rows, w), F32) for rows, _, w in state_dims] + [
        sds((B, HEADS, S, HEAD_W), BF16),
        sds((B, HEADS, S, HEAD_W), BF16),
        sds(vshape, BF16),
        sds((B, S, 2 * D_MODEL), BF16),
        sds((B, HEADS, S, qm_w), BF16),
    ]
    out_specs = [st(blk, w) for _, blk, w in state_dims] + [hd(HEAD_W), hd(HEAD_W), vspec,
                                                            tok(2 * D_MODEL), hd(qm_w)]
    if not absorbed:
        out_shape += [sds((B, HEADS, S, MLA_QK_W), BF16), sds(vshape, BF16)]
        out_specs += [hd(MLA_QK_W), vspec]

    outs = pl.pallas_call(
        functools.partial(_inproj_kernel, absorbed=absorbed, aliased=state is not None, tm=tm),
        grid=(B, ns),
        in_specs=in_specs,
        out_specs=out_specs,
        out_shape=out_shape,
        input_output_aliases=aliases,
        compiler_params=_cparams(2),
        name="in_projection_absorbed" if absorbed else "in_projection",
    )(*ins)
    return tuple(outs[:N_STATE]), tuple(outs[N_STATE:])


def _lambda(lq1_ref, lk1_ref, lq2_ref, lk2_ref, lam_init):
    a = jnp.sum(lq1_ref[...] * lk1_ref[...], axis=-1, keepdims=True)
    b = jnp.sum(lq2_ref[...] * lk2_ref[...], axis=-1, keepdims=True)
    return jnp.exp(a) - jnp.exp(b) + lam_init


def _diff_finish(o1, o2, lam, g, lam_init):
    o = o1 - lam * o2
    return _rms(o, g) * (1.0 - lam_init)


def _key_blocks(qi, tq, tk):
    n_full = (tq * qi) // tk
    blocks = [(tk * jj, tk) for jj in range(n_full)]
    blocks.append((tk * n_full, tq * (qi + 1) - tk * n_full))
    return blocks


def _prompt_attn_kernel(*refs, tq, tk, lookahead, n_q, n_heads, diff, lam_init):
    if diff:
        lq1_ref, lk1_ref, lq2_ref, lk2_ref, g_ref, q_ref, k_ref, vt_ref, o_ref = refs
    else:
        q_ref, k_ref, vt_ref, o_ref = refs
    n_br = 2 if diff else 1
    queries = {}
    masks = {}

    def branch_queries(hd, qi):
        if (hd, qi) not in queries:
            q = q_ref[0, hd, qi * tq:(qi + 1) * tq, :]
            if diff:
                lane = lax.broadcasted_iota(jnp.int32, q.shape, 1)
                zero = jnp.zeros_like(q)
                queries[hd, qi] = (jnp.where(lane < DA_HEAD_DIM, q, zero),
                                   jnp.where(lane >= DA_HEAD_DIM, q, zero))
            else:
                queries[hd, qi] = (q,)
        return queries[hd, qi]

    def diag_mask():
        if not masks:
            kc = lax.broadcasted_iota(jnp.int32, (tq, tq), 0) // CHUNK
            qc = lax.broadcasted_iota(jnp.int32, (tq, tq), 1) // CHUNK
            masks[tq] = kc <= qc
        return masks[tq]

    def scores(task):
        hd, qi, start, size, last, b = task
        s = _dot_nt(k_ref[0, hd, start:start + size, :], branch_queries(hd, qi)[b])
        if not last:
            return s
        tail = jnp.where(diag_mask(), s[size - tq:], MASK_VALUE)
        return tail if size == tq else jnp.concatenate([s[:size - tq], tail], axis=0)

    tasks = []
    for hd in range(n_heads):
        for qi in range(n_q):
            blocks = _key_blocks(qi, tq, tk)
            for start, size in blocks:
                for b in range(n_br):
                    tasks.append((hd, qi, start, size, (start, size) == blocks[-1], b))
    m = [None] * n_br
    acc = [None] * n_br
    pending = [scores(task) for task in tasks[:lookahead]]
    for n, (hd, qi, start, size, last, b) in enumerate(tasks):
        s = pending.pop(0)
        if n + lookahead < len(tasks):
            pending.append(scores(tasks[n + lookahead]))
        vt = vt_ref[0, hd, :, start:start + size]
        s_max = jnp.max(s, axis=0, keepdims=True)
        if start == 0:
            m[b] = s_max
            acc[b] = _dot(vt, jnp.exp2(s - s_max).astype(BF16))
        else:
            m_new = jnp.maximum(m[b], s_max)
            acc[b] = jnp.exp2(m[b] - m_new) * acc[b] + _dot(vt, jnp.exp2(s - m_new).astype(BF16))
            m[b] = m_new
        if last and b == n_br - 1:
            outs = [a[:HEAD_W] / a[HEAD_W:HEAD_W + 1] for a in acc]
            if diff:
                lam = _lambda(lq1_ref, lk1_ref, lq2_ref, lk2_ref, lam_init)
                o_t = outs[0] - lam * outs[1]
                o_t = (o_t * lax.rsqrt(jnp.mean(o_t * o_t, axis=0, keepdims=True) + EPS)
                       * g_ref[...] * (1.0 - lam_init))
            else:
                o_t = outs[0]
            o_ref[0, qi * tq:(qi + 1) * tq, hd * HEAD_W:(hd + 1) * HEAD_W] = o_t.T.astype(BF16)


ATTN_TQ = 256
ATTN_TK = 512
ATTN_LOOKAHEAD = 6
ATTN_HEADS_PER_STEP = 4


def _prompt_attention(q, k, vt, small, *, diff, lam_init, name):
    B, H, S, dq = q.shape
    tq, tk = min(S, ATTN_TQ), min(S, ATTN_TK)
    hps = ATTN_HEADS_PER_STEP
    assert S % tq == 0 and S % tk == 0 and tk % tq == 0 and tq % CHUNK == 0 and H % hps == 0
    qk = pl.BlockSpec((1, hps, S, dq), lambda b, h: (b, h, 0, 0))
    return pl.pallas_call(
        functools.partial(_prompt_attn_kernel, tq=tq, tk=tk, lookahead=ATTN_LOOKAHEAD,
                          n_q=S // tq, n_heads=hps, diff=diff, lam_init=lam_init),
        grid=(B, H // hps),
        in_specs=[_const_spec(a.shape) for a in small]
        + [qk, qk, pl.BlockSpec((1, hps, V_ROWS, S), lambda b, h: (b, h, 0, 0))],
        out_specs=pl.BlockSpec((1, S, hps * HEAD_W), lambda b, h: (b, 0, h)),
        out_shape=jax.ShapeDtypeStruct((B, S, D_MODEL), BF16),
        compiler_params=_cparams(2),
        name=name,
    )(*small, q, k, vt)


def _prompt_diff_attention(q, k, vt, lw, lam_init):
    small = [lw["lam_q1"], lw["lam_k1"], lw["lam_q2"], lw["lam_k2"], lw["g_da_col"]]
    return _prompt_attention(q, k, vt, small, diff=True, lam_init=lam_init, name="diff_attention")


def _prompt_mla_attention(q, k, vt):
    return _prompt_attention(q, k, vt, [], diff=False, lam_init=None, name="mla_attention")


def _two_part_softmax(s_c, s_n):
    m = jnp.maximum(jnp.max(s_c, axis=-1, keepdims=True), jnp.max(s_n, axis=-1, keepdims=True))
    p_c = jnp.exp2(s_c - m)
    p_n = jnp.exp2(s_n - m)
    l = jnp.sum(p_c, axis=-1, keepdims=True) + jnp.sum(p_n, axis=-1, keepdims=True)
    return p_c.astype(BF16), p_n.astype(BF16), l


def _diff_decode_kernel(lq1_ref, lk1_ref, lq2_ref, lk2_ref, g_ref, q_ref, kc_ref, vc_ref,
                        kn_ref, vn_ref, o_ref, *, lam_init, t_cache):
    lam = _lambda(lq1_ref, lk1_ref, lq2_ref, lk2_ref, lam_init)
    for h in range(HEADS):
        q = q_ref[0, h]
        lane = lax.broadcasted_iota(jnp.int32, q.shape, 1)
        zero = jnp.zeros_like(q)
        kc = kc_ref[0, 0, pl.ds(h, t_cache, stride=HEADS), :].astype(BF16)
        vc = vc_ref[0, 0, pl.ds(h, t_cache, stride=HEADS), :].astype(BF16)
        kn = kn_ref[0, h]
        vn = vn_ref[0, h]
        outs = []
        for qq in (jnp.where(lane < DA_HEAD_DIM, q, zero), jnp.where(lane >= DA_HEAD_DIM, q, zero)):
            p_c, p_n, l = _two_part_softmax(_dot_nt(qq, kc), _dot_nt(qq, kn))
            outs.append((_dot(p_c, vc) + _dot(p_n, vn)) / l)
        o_ref[0, :, h * HEAD_W:(h + 1) * HEAD_W] = _diff_finish(
            outs[0], outs[1], lam, g_ref[...], lam_init).astype(BF16)


def _sample_diff_attention(q, cache_k, cache_v, k_new, v_new, layer, lw, lam_init):
    B = cache_k.shape[1]
    H = q.shape[1]
    S = q.shape[2] // B
    rows = cache_k.shape[2]
    small = [lw["lam_q1"], lw["lam_k1"], lw["lam_q2"], lw["lam_k2"], lw["g_da_head"]]
    new = pl.BlockSpec((1, H, S, HEAD_W), lambda b: (0, 0, b, 0))
    cache = pl.BlockSpec((1, 1, rows, HEAD_W), lambda b: (layer, b, 0, 0))
    return pl.pallas_call(
        functools.partial(_diff_decode_kernel, lam_init=lam_init, t_cache=rows // H),
        grid=(B,),
        in_specs=[_const_spec(a.shape) for a in small] + [new, cache, cache, new, new],
        out_specs=pl.BlockSpec((1, S, D_MODEL), lambda b: (b, 0, 0)),
        out_shape=jax.ShapeDtypeStruct((B, S, D_MODEL), BF16),
        compiler_params=_cparams(1),
        name="diff_attention_decode",
    )(*small, q, cache_k, cache_v, k_new, v_new)


def _mla_decode_kernel(q_ref, cc_ref, rc_ref, cn_ref, rn_ref, wuv_ref, o_ref, *, s_len):
    q = q_ref[0].reshape(HEADS * s_len, q_ref.shape[-1])
    ql = q[:, :KV_LORA]
    qr = q[:, KV_LORA:KV_LORA + ROPE_DIM]
    cc = cc_ref[0, 0].astype(BF16)
    cn = cn_ref[0, 0].astype(BF16)
    s_c = _dot_nt(ql, cc) + _dot_nt(qr, rc_ref[0, 0].astype(BF16))
    s_n = _dot_nt(ql, cn) + _dot_nt(qr, rn_ref[0, 0].astype(BF16))
    p_c, p_n, l = _two_part_softmax(s_c, s_n)
    o_lat = ((_dot(p_c, cc) + _dot(p_n, cn)) / l).astype(BF16)
    for i in range(HEADS):
        o_ref[0, :, i * HEAD_W:(i + 1) * HEAD_W] = _dot(
            o_lat[i * s_len:(i + 1) * s_len], wuv_ref[i]).astype(BF16)


def _sample_mla_attention(q_abs, cache_ckv, cache_kr, new_ckv, new_kr, layer, lw):
    B = cache_ckv.shape[1]
    _, H, rows, W = q_abs.shape
    S = rows // B
    cached = lambda a: pl.BlockSpec((1, 1) + a.shape[2:], lambda b: (layer, b, 0, 0))
    new = lambda a: pl.BlockSpec((1, 1, S, a.shape[3]), lambda b: (layer, 0, b, 0))
    return pl.pallas_call(
        functools.partial(_mla_decode_kernel, s_len=S),
        grid=(B,),
        in_specs=[pl.BlockSpec((1, H, S, W), lambda b: (0, 0, b, 0)),
                  cached(cache_ckv), cached(cache_kr), new(new_ckv), new(new_kr),
                  _const_spec(lw["w_uv"].shape)],
        out_specs=pl.BlockSpec((1, S, D_MODEL), lambda b: (b, 0, 0)),
        out_shape=jax.ShapeDtypeStruct((B, S, D_MODEL), BF16),
        compiler_params=_cparams(1),
        name="mla_attention_decode",
    )(q_abs, cache_ckv, cache_kr, new_ckv, new_kr, lw["w_uv"])


F32_SUBLANES = 8


def _merge_ffn_kernel(x_ref, oda_ref, omla_ref, gate_ref, prev_ref, wo_ref, gffn_ref, wup_ref,
                      wconv_ref, bconv_ref, wdown_ref, gfin_ref, y_ref, conv_ref,
                      carry, xbuf, act, *, tm, final):
    sub = F32_SUBLANES
    nv = tm // sub
    n_slab = D_MODEL // HEAD_W

    @pl.when(pl.program_id(1) == 0)
    def _():
        carry[...] = prev_ref[0]

    def to_slabs(val):
        for k in range(n_slab):
            xbuf[k] = val[:, k * HEAD_W:(k + 1) * HEAD_W]

    def from_slabs(row_tile):
        return jnp.concatenate(
            [jnp.concatenate([row_tile(k, c) for k in range(n_slab)], axis=1) for c in range(nv)],
            axis=0)

    gates = gate_ref[0]
    merged = gates[:, :D_MODEL] * oda_ref[0] + gates[:, D_MODEL:] * omla_ref[0]
    to_slabs(x_ref[0] + _dot(merged, wo_ref[...]))
    x1 = from_slabs(lambda k, c: xbuf[k, pl.ds(c, sub, stride=nv), :])
    h = _rms(x1, gffn_ref[...]).astype(BF16)
    first_row = lax.broadcasted_iota(jnp.int32, (sub, FF_CHUNK), 0) == 0

    def conv_cols(c0):
        cols = slice(c0, c0 + FF_CHUNK)
        u = _dot(h, wup_ref[:, cols])
        e1 = jnp.where(first_row, carry[1:2, cols], pltpu.roll(u[tm - sub:], 1, 0))
        e2 = jnp.where(first_row, carry[0:1, cols], pltpu.roll(u[tm - 2 * sub:tm - sub], 1, 0))
        prev1 = jnp.concatenate([e1, u[:tm - sub]], axis=0)
        prev2 = jnp.concatenate([e2, e1, u[:tm - 2 * sub]], axis=0)
        c = (bconv_ref[:, cols] + wconv_ref[0:1, cols] * prev2 + wconv_ref[1:2, cols] * prev1
             + wconv_ref[2:3, cols] * u)
        tail = jnp.concatenate([u[tm - sub - 1:tm - sub], u[tm - 1:tm]], axis=0)
        carry[:, cols] = tail
        conv_ref[0, :, cols] = tail
        return c

    for j in range(N_FF_CHUNKS):
        ca = conv_cols(j * FF_CHUNK)
        cb = conv_cols(D_FF + j * FF_CHUNK)
        act[:, j * FF_CHUNK:(j + 1) * FF_CHUNK] = (jax.nn.silu(ca) * cb).astype(BF16)

    x2 = x1 + _dot(act[...], wdown_ref[...])
    if final:
        x2 = _rms(x2, gfin_ref[...])
    to_slabs(x2)

    def natural_tile(k, tile):
        t0 = tile * sub
        if nv % sub == 0:
            return xbuf[k, pl.ds((t0 % nv) * sub + t0 // nv, sub, stride=sub), :]
        return jnp.concatenate(
            [xbuf[k, pl.ds(t0 // nv + g, nv, stride=sub), :] for g in range(sub // nv)], axis=0)

    y_ref[0] = from_slabs(natural_tile)


def _merge_ffn(x, o_da, o_mla, gates, prev_conv, lw, g_final, *, final, tm):
    B, S, _ = x.shape
    nv = tm // F32_SUBLANES
    assert S % tm == 0 and tm % BF16_SUBLANES == 0 and nv >= CONV_W - 1
    assert nv % F32_SUBLANES == 0 or F32_SUBLANES % nv == 0
    tok = lambda w: pl.BlockSpec((1, tm, w), lambda b, s: (b, s, 0))
    state = pl.BlockSpec((1, CONV_W - 1, 2 * D_FF), lambda b, s: (b, 0, 0))
    consts = [lw["w_o"], lw["g_ffn"], lw["w_up"], lw["w_conv"], lw["b_conv"], lw["w_down"], g_final]
    return pl.pallas_call(
        functools.partial(_merge_ffn_kernel, tm=tm, final=final),
        grid=(B, S // tm),
        in_specs=[tok(D_MODEL), tok(D_MODEL), tok(D_MODEL), tok(2 * D_MODEL), state]
        + [_const_spec(a.shape) for a in consts],
        out_specs=[tok(D_MODEL), state],
        out_shape=[jax.ShapeDtypeStruct((B, S, D_MODEL), F32),
                   jax.ShapeDtypeStruct((B, CONV_W - 1, 2 * D_FF), F32)],
        scratch_shapes=[pltpu.VMEM((CONV_W - 1, 2 * D_FF), F32),
                        pltpu.VMEM((D_MODEL // HEAD_W, tm, HEAD_W), F32),
                        pltpu.VMEM((tm, D_FF), BF16)],
        compiler_params=_cparams(2),
        name="merge_ffn_final" if final else "merge_ffn",
    )(x, o_da, o_mla, gates, prev_conv, *consts)


def _layer_weights(l, g_attn, w_in, b_gate, lam_q1, lam_k1, lam_q2, lam_k2, g_da_head, g_cq, w_uq,
                   g_ckv, w_uk, w_uv, w_o, g_ffn, w_up, w_conv, b_conv, w_down):
    row = lambda a: a[l].reshape(1, -1).astype(F32)
    w = w_in[l]
    o1 = D_MODEL
    o2 = o1 + D_MODEL
    o3 = o2 + D_MODEL
    o4 = o3 + Q_LORA
    o5 = o4 + KV_LORA
    o6 = o5 + ROPE_DIM
    uq = w_uq[l]
    pad_rope = lambda a: jnp.pad(a, [(0, 0)] * (a.ndim - 1) + [(0, HEAD_W - ROPE_DIM)])
    return {
        "g_attn": row(g_attn),
        "w_q": w[:, :o1].astype(BF16),
        "w_k": w[:, o1:o2].astype(BF16),
        "w_v": w[:, o2:o3].astype(BF16),
        "w_cq": w[:, o3:o4].astype(BF16),
        "w_ckv": w[:, o4:o5].astype(BF16),
        "w_kr": pad_rope(w[:, o5:o6]).astype(BF16),
        "w_g": w[:, o6:].astype(BF16),
        "b_gate": row(b_gate),
        "g_cq": row(g_cq),
        "g_ckv": row(g_ckv),
        "w_uqn": uq[:, :, :NOPE_DIM].reshape(Q_LORA, HEADS * NOPE_DIM).astype(BF16),
        "w_uqr": pad_rope(uq[:, :, NOPE_DIM:]).reshape(Q_LORA, HEADS * HEAD_W).astype(BF16),
        "w_ukf": jnp.transpose(w_uk[l], (1, 0, 2)).reshape(KV_LORA, HEADS * NOPE_DIM).astype(BF16),
        "w_uvf": jnp.transpose(w_uv[l], (1, 0, 2)).reshape(KV_LORA, HEADS * V_HEAD_DIM).astype(BF16),
        "w_ukt": jnp.transpose(w_uk[l], (0, 2, 1)).astype(BF16),
        "w_uv": w_uv[l].astype(BF16),
        "lam_q1": row(lam_q1), "lam_k1": row(lam_k1), "lam_q2": row(lam_q2), "lam_k2": row(lam_k2),
        "g_da_head": row(g_da_head),
        "g_da_col": g_da_head[l].reshape(-1, 1).astype(F32),
        "w_o": w_o[l].astype(BF16),
        "g_ffn": row(g_ffn),
        "w_up": w_up[l].astype(BF16),
        "w_conv": w_conv[l].astype(F32),
        "b_conv": row(b_conv),
        "w_down": w_down[l].astype(BF16),
    }


def _all_keys_visible(q_pos, k_pos):
    return bool(np.all((k_pos[None, :] // CHUNK) <= (q_pos[:, None] // CHUNK)))


def kernel(x_prompt, x_sample, cache_dk, cache_dv, cache_ckv, cache_krope, state_conv, g_attn, w_in, b_gate, lam_q1, lam_k1, lam_q2, lam_k2, g_da_head, g_cq, w_uq, g_ckv, w_uk, w_uv, w_o, g_ffn, w_up, w_conv, b_conv, w_down, g_final):
    depth = w_in.shape[0]
    B, S, _ = x_prompt.shape
    Bs, Ss, _ = x_sample.shape
    past = cache_dk.shape[2]
    assert _all_keys_visible(past + np.arange(Ss), np.arange(past + Ss))

    gfin = g_final.reshape(1, -1).astype(F32)
    pos_p = jnp.arange(S, dtype=jnp.int32)
    pos_s = past + jnp.arange(Ss, dtype=jnp.int32)
    tm_in = min(S, 512)
    tm_ffn = min(S, 512)
    zero_state = jnp.zeros((B, CONV_W - 1, 2 * D_FF), F32)
    cache_dk2 = cache_dk.reshape(depth, Bs, past * DA_HEADS, DA_V_DIM)
    cache_dv2 = cache_dv.reshape(depth, Bs, past * DA_HEADS, DA_V_DIM)

    yp, ys = x_prompt, x_sample
    state_p = state_s = None
    conv_p, conv_s = [], []
    for l in range(depth):
        lw = _layer_weights(l, g_attn, w_in, b_gate, lam_q1, lam_k1, lam_q2, lam_k2, g_da_head,
                            g_cq, w_uq, g_ckv, w_uk, w_uv, w_o, g_ffn, w_up, w_conv, b_conv, w_down)
        lam_init = 0.8 - 0.6 * math.exp(-0.3 * l)
        final = l == depth - 1

        state_p, (q_da, k_b, v_b, gates, q_m, k_m, v_m) = _in_projection(
            yp, pos_p, lw, l, depth, state_p, absorbed=False, tm=tm_in)
        o_da = _prompt_diff_attention(q_da, k_b, v_b, lw, lam_init)
        o_mla = _prompt_mla_attention(q_m, k_m, v_m)
        yp, cv = _merge_ffn(yp, o_da, o_mla, gates, zero_state, lw, gfin, final=final, tm=tm_ffn)
        conv_p.append(cv)

        state_s, (q_da, k_b, v_b, gates, q_abs) = _in_projection(
            ys.reshape(1, Bs * Ss, D_MODEL), jnp.tile(pos_s, Bs), lw, l, depth, state_s,
            absorbed=True, tm=Bs * Ss)
        o_da = _sample_diff_attention(q_da, cache_dk2, cache_dv2, k_b, v_b, l, lw, lam_init)
        o_mla = _sample_mla_attention(q_abs, cache_ckv, cache_krope, state_s[2], state_s[3], l, lw)
        ys, cv = _merge_ffn(ys, o_da, o_mla, gates.reshape(Bs, Ss, 2 * D_MODEL), state_conv[l], lw,
                            gfin, final=final, tm=Ss)
        conv_s.append(cv)

    dk_p, dv_p, ckv_p, kr_p = state_p
    dk_s, dv_s, ckv_s, kr_s = state_s
    heads = lambda a, b, s: a.reshape(depth, b, s, DA_HEADS, DA_V_DIM)
    seqs = lambda a: a.reshape(depth, Bs, Ss, a.shape[-1])
    return (yp, ys, heads(dk_p, B, S), heads(dv_p, B, S), ckv_p, kr_p, jnp.stack(conv_p),
            heads(dk_s, Bs, Ss), heads(dv_s, Bs, Ss), seqs(ckv_s), seqs(kr_s), jnp.stack(conv_s))
```
